```python
import math
import jax
import jax.numpy as jnp
from jax import lax
import numpy as np

D_MODEL = 1024
BATCH = 4
SEQ = 4096
DEPTH = 2
DEC_BATCH = 32
DEC_SEQ = 1
PAST_LEN = 8192
PAGE_SIZE = 128

A_HEADS = 4
A_DK = 64
A_DV = 2 * A_DK
B_HEADS = 8
B_DH = 64
IDX_HEADS = 8
IDX_DIM = 64
IDX_SCALE = (IDX_HEADS ** -0.5) * (IDX_DIM ** -0.5)
TOPK_MAX = 256
ROPE_THETA = 500000.0
ROPE_FRACTION = 4
Q_BLOCK = 128
FF_DENSE = 2816
N_EXPERTS = 8
TOP_K_EXPERTS = 2
FF_EXPERT = 3584
N_DENSE = (DEPTH + 1) // 2
N_MOE = DEPTH // 2
MIX_WIDTH = A_HEADS * A_DV + B_HEADS * B_DH
IN_SPLITS = (A_HEADS * 2 * A_DK, A_HEADS * 2 * A_DK, A_HEADS * A_DV,
             B_HEADS * B_DH, B_HEADS * B_DH, B_HEADS * B_DH,
             IDX_HEADS * IDX_DIM, IDX_DIM, IDX_HEADS)
IN_WIDTH = sum(IN_SPLITS)
EPS = 1e-6

kernel_name = 'hymba_diffattn_dsa_adaln_step'


def rms_norm(x, g):
    xf = x.astype(jnp.float32)
    y = xf * lax.rsqrt(jnp.mean(xf * xf, axis=-1, keepdims=True) + EPS)
    return (y * g.astype(jnp.float32)).astype(x.dtype)


def rope_partial(x, pos):
    r = x.shape[-1] // ROPE_FRACTION
    half = r // 2
    inv = ROPE_THETA ** (-jnp.arange(half, dtype=jnp.float32) * 2.0 / r)
    ang = pos.astype(jnp.float32)[:, None] * inv[None, :]
    shape = (pos.shape[0],) + (1,) * (x.ndim - 3) + (half,)
    cos = jnp.cos(ang).reshape(shape)
    sin = jnp.sin(ang).reshape(shape)
    xr = x[..., :r].astype(jnp.float32)
    x1, x2 = xr[..., :half], xr[..., half:]
    rot = jnp.concatenate([x1 * cos - x2 * sin, x2 * cos + x1 * sin], axis=-1).astype(x.dtype)
    return jnp.concatenate([rot, x[..., r:]], axis=-1)


def adaln(c, w_ada, b_ada):
    mod = jax.nn.silu(c) @ w_ada + b_ada
    return [m[:, None, :] for m in jnp.split(mod, 6, axis=-1)]


def diff_lambda(lq1, lk1, lq2, lk2, layer):
    lam_init = 0.8 - 0.6 * math.exp(-0.3 * layer)
    f = jnp.float32
    lam = (jnp.exp(jnp.sum(lq1.astype(f) * lk1.astype(f)))
           - jnp.exp(jnp.sum(lq2.astype(f) * lk2.astype(f))) + lam_init)
    return lam, lam_init


def project_mixers(h, pos, w_in, a_qn, a_kn, b_qn, b_kn):
    bsz, t, _ = h.shape
    offs = [int(o) for o in np.cumsum(IN_SPLITS)[:-1]]
    aq, ak, av, bq, bk, bv, iq, ik, iw = jnp.split(h @ w_in, offs, axis=-1)
    aq = rope_partial(rms_norm(aq.reshape(bsz, t, A_HEADS, 2, A_DK), a_qn), pos)
    ak = rope_partial(rms_norm(ak.reshape(bsz, t, A_HEADS, 2, A_DK), a_kn), pos)
    av = av.reshape(bsz, t, A_HEADS, A_DV)
    bq = rope_partial(rms_norm(bq.reshape(bsz, t, B_HEADS, B_DH), b_qn), pos)
    bk = rope_partial(rms_norm(bk.reshape(bsz, t, B_HEADS, B_DH), b_kn), pos)
    bv = bv.reshape(bsz, t, B_HEADS, B_DH)
    iq = rope_partial(iq.reshape(bsz, t, IDX_HEADS, IDX_DIM), pos)
    ik = rope_partial(ik, pos)
    return (aq, ak, av, bq, bk, bv, iq, ik, iw)


def gather_pages(cache, layer, page_table):
    g = cache[layer, page_table]
    return g.reshape((page_table.shape[0], -1) + cache.shape[3:])


def diff_attn_core(q, k, v, qpos, kpos, lam):
    s = jnp.einsum('bqhmd,bkhmd->bhmqk', q, k).astype(jnp.float32) * (A_DK ** -0.5)
    causal = kpos[None, :] <= qpos[:, None]
    p = jax.nn.softmax(jnp.where(causal, s, -jnp.inf), axis=-1)
    attn = p[:, :, 0] - lam * p[:, :, 1]
    return jnp.einsum('bhqk,bkhd->bqhd', attn.astype(v.dtype), v)


def diff_attn_prompt(q, k, v, pos, lam):
    bsz, t = q.shape[:2]
    nb = t // Q_BLOCK
    qb = q.reshape((bsz, nb, Q_BLOCK) + q.shape[2:]).swapaxes(0, 1)
    pb = pos.reshape(nb, Q_BLOCK)
    out = lax.map(lambda a: diff_attn_core(a[0], k, v, a[1], pos, lam), (qb, pb))
    return out.swapaxes(0, 1).reshape((bsz, t) + out.shape[3:])


def diff_attn_sample(q, k_new, v_new, cache_k, cache_v, layer, page_table, pos, lam):
    k = jnp.concatenate([gather_pages(cache_k, layer, page_table), k_new], axis=1)
    v = jnp.concatenate([gather_pages(cache_v, layer, page_table), v_new], axis=1)
    kpos = jnp.arange(k.shape[1], dtype=jnp.int32)
    return diff_attn_core(q, k, v, pos, kpos, lam)


def dsa_core(q, iq, iw, qpos, ik_all, kpos, topk, gather_kv):
    logits = jnp.einsum('bqhd,bsd->bqhs', iq, ik_all).astype(jnp.float32)
    score = jnp.einsum('bqhs,bqh->bqs', jax.nn.relu(logits), iw.astype(jnp.float32)) * IDX_SCALE
    score = jnp.where(kpos[None, None, :] <= qpos[None, :, None], score, -jnp.inf)
    _, idx = lax.top_k(score, topk)
    valid = idx <= qpos[None, :, None]
    k_sel, v_sel = gather_kv(idx)
    s = jnp.einsum('bqhd,bqkhd->bqhk', q, k_sel).astype(jnp.float32) * (B_DH ** -0.5)
    p = jax.nn.softmax(jnp.where(valid[:, :, None, :], s, -jnp.inf), axis=-1)
    return jnp.einsum('bqhk,bqkhd->bqhd', p.astype(v_sel.dtype), v_sel)


def dsa_prompt(q, k, v, iq, ik, iw, pos):
    bsz, t = q.shape[:2]
    topk = min(TOPK_MAX, t // 4)
    nb = t // Q_BLOCK
    blk = lambda a: a.reshape((bsz, nb, Q_BLOCK) + a.shape[2:]).swapaxes(0, 1)
    take = jax.vmap(lambda rows, ids: rows[ids])

    def gather_kv(idx):
        return take(k, idx), take(v, idx)

    def one(a):
        return dsa_core(a[0], a[1], a[2], a[3], ik, pos, topk, gather_kv)

    out = lax.map(one, (blk(q), blk(iq), blk(iw), pos.reshape(nb, Q_BLOCK)))
    return out.swapaxes(0, 1).reshape((bsz, t) + out.shape[3:])


def dsa_sample(q, k_new, v_new, iq, ik_new, iw, cache_k, cache_v, cache_ik, layer, page_table, pos):
    nbat, t = q.shape[:2]
    past = page_table.shape[1] * PAGE_SIZE
    n_keys = past + t
    topk = min(TOPK_MAX, n_keys // 4)
    ik_all = jnp.concatenate([gather_pages(cache_ik, layer, page_table), ik_new], axis=1)
    kpos = jnp.arange(n_keys, dtype=jnp.int32)
    bidx = jnp.arange(nbat)[:, None, None]

    def gather_kv(idx):
        is_new = (idx >= past)[..., None, None]
        pc = jnp.minimum(idx, past - 1)
        phys = page_table[bidx, pc // PAGE_SIZE]
        off = pc % PAGE_SIZE
        ni = jnp.clip(idx - past, 0, t - 1)
        k_sel = jnp.where(is_new, k_new[bidx, ni], cache_k[layer, phys, off])
        v_sel = jnp.where(is_new, v_new[bidx, ni], cache_v[layer, phys, off])
        return k_sel, v_sel

    return dsa_core(q, iq, iw, pos, ik_all, kpos, topk, gather_kv)


def swiglu(h, w1, w3, w2):
    return (jax.nn.silu(h @ w1) * (h @ w3)) @ w2


def moe_swiglu(h, w_router, b_router, w1, w3, w2):
    logits = (h @ w_router).astype(jnp.float32) + b_router.astype(jnp.float32)
    top_val, top_idx = lax.top_k(logits, TOP_K_EXPERTS)
    gate = jax.nn.softmax(top_val, axis=-1)
    combine = jnp.sum(jax.nn.one_hot(top_idx, N_EXPERTS, dtype=jnp.float32) * gate[..., None], axis=-2)
    combine = combine.astype(h.dtype)
    out = jnp.zeros_like(h)
    for e in range(N_EXPERTS):
        out = out + combine[..., e:e + 1] * swiglu(h, w1[e], w3[e], w2[e])
    return out


def setup_inputs(seed: int = 0) -> dict:
    key = jax.random.key(seed)
    keys = jax.random.split(key, 48)
    counter = [0]
    f32 = jnp.float32

    def nk():
        counter[0] += 1
        return keys[counter[0] - 1]

    def nrm(shape, scale):
        return jax.random.normal(nk(), shape, f32) * scale

    n_pages = PAST_LEN // PAGE_SIZE
    n_used = DEC_BATCH * n_pages
    n_pool = n_used + max(1, n_used // 4)
    perm = jax.random.permutation(nk(), n_pool)
    page_table = perm[:n_used].reshape(DEC_BATCH, n_pages).astype(jnp.int32)

    d = D_MODEL
    return {
        'x_prompt': nrm((BATCH, SEQ, d), 1.0),
        'x_sample': nrm((DEC_BATCH, DEC_SEQ, d), 1.0),
        'cache_a_k': nrm((DEPTH, n_pool, PAGE_SIZE, A_HEADS, 2, A_DK), 1.0),
        'cache_a_v': nrm((DEPTH, n_pool, PAGE_SIZE, A_HEADS, A_DV), 1.0),
        'cache_b_k': nrm((DEPTH, n_pool, PAGE_SIZE, B_HEADS, B_DH), 1.0),
        'cache_b_v': nrm((DEPTH, n_pool, PAGE_SIZE, B_HEADS, B_DH), 1.0),
        'cache_b_ik': nrm((DEPTH, n_pool, PAGE_SIZE, IDX_DIM), 1.0),
        'page_table': page_table,
        'c_prompt': nrm((BATCH, d), 1.0),
        'c_sample': nrm((DEC_BATCH, d), 1.0),
        'w_ada': nrm((DEPTH, d, 6 * d), 0.5 * d ** -0.5),
        'b_ada': nrm((DEPTH, 6 * d), 0.02),
        'g_mix': 1.0 + nrm((DEPTH, d), 0.02),
        'g_ffn': 1.0 + nrm((DEPTH, d), 0.02),
        'w_in': nrm((DEPTH, d, IN_WIDTH), d ** -0.5),
        'w_out': nrm((DEPTH, MIX_WIDTH, d), MIX_WIDTH ** -0.5),
        'a_q_norm': 1.0 + nrm((DEPTH, A_DK), 0.02),
        'a_k_norm': 1.0 + nrm((DEPTH, A_DK), 0.02),
        'b_q_norm': 1.0 + nrm((DEPTH, B_DH), 0.02),
        'b_k_norm': 1.0 + nrm((DEPTH, B_DH), 0.02),
        'lam_q1': nrm((DEPTH, A_DK), 0.1),
        'lam_k1': nrm((DEPTH, A_DK), 0.1),
        'lam_q2': nrm((DEPTH, A_DK), 0.1),
        'lam_k2': nrm((DEPTH, A_DK), 0.1),
        'a_sub_norm': 1.0 + nrm((DEPTH, A_DV), 0.02),
        'w1_dense': nrm((N_DENSE, d, FF_DENSE), d ** -0.5),
        'w3_dense': nrm((N_DENSE, d, FF_DENSE), d ** -0.5),
        'w2_dense': nrm((N_DENSE, FF_DENSE, d), FF_DENSE ** -0.5),
        'w_router': nrm((N_MOE, d, N_EXPERTS), d ** -0.5),
        'b_router': nrm((N_MOE, N_EXPERTS), 0.01),
        'w1_exp': nrm((N_MOE, N_EXPERTS, d, FF_EXPERT), d ** -0.5),
        'w3_exp': nrm((N_MOE, N_EXPERTS, d, FF_EXPERT), d ** -0.5),
        'w2_exp': nrm((N_MOE, N_EXPERTS, FF_EXPERT, d), FF_EXPERT ** -0.5),
    }


def reference(x_prompt, x_sample, cache_a_k, cache_a_v, cache_b_k, cache_b_v, cache_b_ik,
              page_table, c_prompt, c_sample, w_ada, b_ada, g_mix, g_ffn, w_in, w_out,
              a_q_norm, a_k_norm, b_q_norm, b_k_norm, lam_q1, lam_k1, lam_q2, lam_k2,
              a_sub_norm, w1_dense, w3_dense, w2_dense, w_router, b_router,
              w1_exp, w3_exp, w2_exp):
    past = page_table.shape[1] * PAGE_SIZE
    pos_p = jnp.arange(x_prompt.shape[1], dtype=jnp.int32)
    pos_s = past + jnp.arange(x_sample.shape[1], dtype=jnp.int32)

    def run_layer(x, c, pos, l, attend):
        lam, lam_init = diff_lambda(lam_q1[l], lam_k1[l], lam_q2[l], lam_k2[l], l)
        shift1, scale1, gate1, shift2, scale2, gate2 = adaln(c, w_ada[l], b_ada[l])
        h = rms_norm(x, g_mix[l]) * (1.0 + scale1) + shift1
        proj = project_mixers(h, pos, w_in[l], a_q_norm[l], a_k_norm[l], b_q_norm[l], b_k_norm[l])
        a_o, b_o = attend(proj, lam)
        bsz, t = x.shape[:2]
        a_o = rms_norm(a_o, a_sub_norm[l]) * (1.0 - lam_init)
        mix = jnp.concatenate([a_o.reshape(bsz, t, -1), b_o.reshape(bsz, t, -1)], axis=-1)
        x = x + gate1 * (mix @ w_out[l])
        h = rms_norm(x, g_ffn[l]) * (1.0 + scale2) + shift2
        j = l // 2
        if l % 2 == 0:
            f = swiglu(h, w1_dense[j], w3_dense[j], w2_dense[j])
        else:
            f = moe_swiglu(h, w_router[j], b_router[j], w1_exp[j], w3_exp[j], w2_exp[j])
        return x + gate2 * f, proj

    xp, xs = x_prompt, x_sample
    rows_p, rows_s = [], []
    for l in range(DEPTH):
        def attend_prompt(p, lam):
            aq, ak, av, bq, bk, bv, iq, ik, iw = p
            return (diff_attn_prompt(aq, ak, av, pos_p, lam),
                    dsa_prompt(bq, bk, bv, iq, ik, iw, pos_p))

        def attend_sample(p, lam, layer=l):
            aq, ak, av, bq, bk, bv, iq, ik, iw = p
            return (diff_attn_sample(aq, ak, av, cache_a_k, cache_a_v, layer, page_table, pos_s, lam),
                    dsa_sample(bq, bk, bv, iq, ik, iw, cache_b_k, cache_b_v, cache_b_ik,
                               layer, page_table, pos_s))

        xp, proj_p = run_layer(xp, c_prompt, pos_p, l, attend_prompt)
        xs, proj_s = run_layer(xs, c_sample, pos_s, l, attend_sample)
        rows_p.append([proj_p[1], proj_p[2], proj_p[4], proj_p[5], proj_p[7]])
        rows_s.append([proj_s[1], proj_s[2], proj_s[4], proj_s[5], proj_s[7]])

    a_k_p = jnp.stack([r[0] for r in rows_p])
    a_v_p = jnp.stack([r[1] for r in rows_p])
    b_k_p = jnp.stack([r[2] for r in rows_p])
    b_v_p = jnp.stack([r[3] for r in rows_p])
    b_ik_p = jnp.stack([r[4] for r in rows_p])
    a_k_s = jnp.stack([r[0] for r in rows_s])
    a_v_s = jnp.stack([r[1] for r in rows_s])
    b_k_s = jnp.stack([r[2] for r in rows_s])
    b_v_s = jnp.stack([r[3] for r in rows_s])
    b_ik_s = jnp.stack([r[4] for r in rows_s])
    return (xp, xs, a_k_p, a_v_p, b_k_p, b_v_p, b_ik_p, a_k_s, a_v_s, b_k_s, b_v_s, b_ik_s)
```

```python
import functools
import math

import jax
import jax.numpy as jnp
from jax import lax
from jax.experimental import pallas as pl
from jax.experimental.pallas import tpu as pltpu

F32 = jnp.float32
BF16 = jnp.bfloat16
HIGHEST = lax.Precision.HIGHEST

A_HEADS = 4
A_DK = 64
A_DV = 2 * A_DK
B_HEADS = 8
B_DH = 64
IDX_HEADS = 8
IDX_DIM = 64
IDX_SCALE = (IDX_HEADS ** -0.5) * (IDX_DIM ** -0.5)
TOPK_MAX = 256
ROPE_THETA = 500000.0
ROPE_FRACTION = 4
N_EXPERTS = 8
EPS = 1e-6

HEAD_DIM = 64
ROPE_HALF = HEAD_DIM // ROPE_FRACTION // 2
SLAB = 512
N_SLABS = 7
MISC = 128
IW_LANE0 = IDX_DIM
NPROJ = N_SLABS * SLAB + MISC
LANES = 128
INT_MIN = -2 ** 31
INT_MAX = 2 ** 31 - 1
NEG = -1e30
VMEM_LIMIT = 56 * 1024 * 1024


def _dot(a, b):
    return jnp.dot(a, b, preferred_element_type=F32)


def _dot_nt(a, b):
    return lax.dot_general(a, b, (((1,), (1,)), ((), ())), preferred_element_type=F32)


def _cparams(*sem):
    return pltpu.CompilerParams(dimension_semantics=sem, vmem_limit_bytes=VMEM_LIMIT)


def _pick_tile(n, target, mult):
    best = None
    for t in range(mult, min(n, target) + 1, mult):
        if n % t == 0:
            best = t
    return best if best is not None else n


def _ada_kernel(c_ref, w_ref, b_ref, o_ref):
    c = c_ref[...]
    o_ref[...] = jnp.dot(jax.nn.silu(c), w_ref[...], precision=HIGHEST,
                         preferred_element_type=F32) + b_ref[...]


def _ada(c_all, w_ada, b_ada):
    depth, d, n = w_ada.shape
    r = c_all.shape[0]
    tn = _pick_tile(n, 1536, LANES)
    return pl.pallas_call(
        _ada_kernel,
        out_shape=jax.ShapeDtypeStruct((depth, r, n), F32),
        grid=(depth, n // tn),
        in_specs=[pl.BlockSpec((r, d), lambda l, j: (0, 0)),
                  pl.BlockSpec((None, d, tn), lambda l, j: (l, 0, j)),
                  pl.BlockSpec((None, 1, tn), lambda l, j: (l, 0, j))],
        out_specs=pl.BlockSpec((None, r, tn), lambda l, j: (l, 0, j)),
        compiler_params=_cparams("parallel", "parallel"),
        name="adaln",
    )(c_all, w_ada, b_ada.reshape(depth, 1, n))


def _rope_tables(pos):
    r = HEAD_DIM // ROPE_FRACTION
    inv = ROPE_THETA ** (-jnp.arange(ROPE_HALF, dtype=F32) * 2.0 / r)
    ang = pos.astype(F32)[:, None] * inv[None, :]
    cos, sin = jnp.cos(ang), jnp.sin(ang)
    s = pos.shape[0]
    rest = HEAD_DIM - 2 * ROPE_HALF
    z8 = jnp.zeros((s, ROPE_HALF), F32)
    zr = jnp.zeros((s, rest), F32)
    c = jnp.concatenate([cos, cos, jnp.ones((s, rest), F32)], axis=1)
    s1 = jnp.concatenate([-sin, z8, zr], axis=1)
    s2 = jnp.concatenate([z8, sin, zr], axis=1)
    rep = LANES // HEAD_DIM
    return tuple(jnp.concatenate([t] * rep, axis=1) for t in (c, s1, s2))


def _proj_kernel(x_ref, sc_ref, sh_ref, g_ref, w_ref, gains_ref, rc_ref, rs1_ref, rs2_ref,
                 m64_ref, akf, avf, bkf, bvf, miscf, aqb, akb, avb, bqb, bkb, bvb, iqb, ik2b,
                 h_scr):
    x = x_ref[...]
    h = x * lax.rsqrt(jnp.mean(x * x, axis=-1, keepdims=True) + EPS) * g_ref[...]
    h = h * (1.0 + sc_ref[...]) + sh_ref[...]
    h_scr[...] = h.astype(BF16)

    c1, s1, s2 = rc_ref[...], rs1_ref[...], rs2_ref[...]
    rep = SLAB // LANES
    c4 = jnp.concatenate([c1] * rep, axis=1)
    s14 = jnp.concatenate([s1] * rep, axis=1)
    s24 = jnp.concatenate([s2] * rep, axis=1)
    m64 = m64_ref[...]

    def slab(i, width=SLAB):
        return _dot(h_scr[...], w_ref[:, i * SLAB:i * SLAB + width])

    def qknorm(p, gi):
        sq = p * p
        hi = sq.astype(BF16)
        lo = (sq - hi.astype(F32)).astype(BF16)
        ms = _dot(hi, m64) + _dot(lo, m64)
        return p * lax.rsqrt(ms + EPS) * gains_ref[gi:gi + 1, :]

    def rope(y, c, a, b):
        n = y.shape[-1]
        return y * c + pltpu.roll(y, n - ROPE_HALF, 1) * a + pltpu.roll(y, ROPE_HALF, 1) * b

    aq = rope(qknorm(slab(0), 0), c4, s14, s24) * (A_DK ** -0.5)
    aqb[...] = aq.astype(BF16)
    ak = rope(qknorm(slab(1), 1), c4, s14, s24)
    akf[...] = ak
    akb[...] = ak.astype(BF16)
    av = slab(2)
    avf[...] = av
    avb[...] = av.astype(BF16)
    bq = rope(qknorm(slab(3), 2), c4, s14, s24) * (B_DH ** -0.5)
    bqb[...] = bq.astype(BF16)
    bk = rope(qknorm(slab(4), 3), c4, s14, s24)
    bkf[...] = bk
    bkb[...] = bk.astype(BF16)
    bv = slab(5)
    bvf[...] = bv
    bvb[...] = bv.astype(BF16)
    iq = rope(slab(6), c4, s14, s24)
    iqb[...] = iq.astype(BF16)
    pm = slab(N_SLABS, MISC)
    lane = lax.broadcasted_iota(jnp.int32, pm.shape, 1)
    misc = jnp.where(lane < IDX_DIM, rope(pm, c1, s1, s2), pm)
    miscf[...] = misc
    ik2b[...] = jnp.where(lane < IDX_DIM, misc, pltpu.roll(misc, IDX_DIM, 1)).astype(BF16)


def _mod_spec(arr, tm):
    _, r, d = arr.shape
    if r == 1:
        return pl.BlockSpec((None, 1, d), lambda b, i, *_: (b, 0, 0))
    return pl.BlockSpec((None, tm, d), lambda b, i, *_: (b, i, 0))


def _project(x3, scale, shift, g, w_bf, gains, tabs, m64):
    nb, s, d = x3.shape
    tm = _pick_tile(s, 512, 16)
    rc, rs1, rs2 = tabs
    row = lambda w: pl.BlockSpec((None, tm, w), lambda b, i: (b, i, 0))
    tab = pl.BlockSpec((tm, LANES), lambda b, i: (i, 0))
    full = lambda a: pl.BlockSpec(a.shape, lambda b, i: (0,) * a.ndim)
    f = lambda w: jax.ShapeDtypeStruct((nb, s, w), F32)
    h = lambda w: jax.ShapeDtypeStruct((nb, s, w), BF16)
    return pl.pallas_call(
        _proj_kernel,
        out_shape=(f(SLAB), f(SLAB), f(SLAB), f(SLAB), f(MISC),
                   h(SLAB), h(SLAB), h(SLAB), h(SLAB), h(SLAB), h(SLAB), h(SLAB), h(MISC)),
        grid=(nb, s // tm),
        in_specs=[row(d), _mod_spec(scale, tm), _mod_spec(shift, tm), full(g), full(w_bf),
                  full(gains), tab, tab, tab, full(m64)],
        out_specs=(row(SLAB), row(SLAB), row(SLAB), row(SLAB), row(MISC),
                   row(SLAB), row(SLAB), row(SLAB), row(SLAB), row(SLAB), row(SLAB), row(SLAB),
                   row(MISC)),
        scratch_shapes=[pltpu.VMEM((tm, d), BF16)],
        compiler_params=_cparams("parallel", "parallel"),
        name="in_proj",
    )(x3, scale, shift, g, w_bf, gains, rc, rs1, rs2, m64)


def _lambda(lam_ref, lam_init):
    lv = lam_ref[...]
    a = jnp.sum(lv[0:1] * lv[1:2], axis=-1, keepdims=True)
    b = jnp.sum(lv[2:3] * lv[3:4], axis=-1, keepdims=True)
    return jnp.exp(a) - jnp.exp(b) + lam_init


def _diff_kernel(qi_ref, ki_ref, q_ref, k_ref, v_ref, lam_ref, gain_ref, o_ref,
                 m_ref, l_ref, acc_ref, *, lam_init):
    p_id = pl.program_id(2)
    qi = qi_ref[p_id]
    ki = ki_ref[p_id]

    @pl.when(ki == 0)
    def _():
        m_ref[...] = jnp.full(m_ref.shape, NEG, F32)
        l_ref[...] = jnp.zeros(l_ref.shape, F32)
        acc_ref[...] = jnp.zeros(acc_ref.shape, F32)

    q = q_ref[...]
    lane = lax.broadcasted_iota(jnp.int32, q.shape, 1)
    zero = jnp.zeros_like(q)
    qs = (jnp.where(lane < A_DK, q, zero), jnp.where(lane >= A_DK, q, zero))
    k = k_ref[...]
    v = v_ref[...]

    def update(masked):
        for mp in range(2):
            s = _dot_nt(qs[mp], k)
            if masked:
                r = lax.broadcasted_iota(jnp.int32, s.shape, 0)
                c = lax.broadcasted_iota(jnp.int32, s.shape, 1)
                s = jnp.where(c <= r, s, NEG)
            m_prev = m_ref[mp]
            m_new = jnp.maximum(m_prev, jnp.max(s, axis=-1, keepdims=True))
            alpha = jnp.exp(m_prev - m_new)
            p = jnp.exp(s - m_new)
            l_ref[mp] = alpha * l_ref[mp] + jnp.sum(p, axis=-1, keepdims=True)
            acc_ref[mp] = alpha * acc_ref[mp] + _dot(p.astype(BF16), v)
            m_ref[mp] = m_new

    @pl.when(ki < qi)
    def _():
        update(False)

    @pl.when(ki == qi)
    def _():
        update(True)
        lam = _lambda(lam_ref, lam_init)
        o = acc_ref[0] / l_ref[0] - lam * (acc_ref[1] / l_ref[1])
        y = o * lax.rsqrt(jnp.mean(o * o, axis=-1, keepdims=True) + EPS) * gain_ref[...]
        o_ref[...] = (y * (1.0 - lam_init)).astype(o_ref.dtype)


def _diff_prompt(aq, ak, av, lamvec, gain, lam_init):
    nb, s, _ = aq.shape
    t = _pick_tile(s, 512, 16)
    nq = s // t
    pairs = [(i, j) for i in range(nq) for j in range(i + 1)]
    qi = jnp.asarray([p[0] for p in pairs], jnp.int32)
    ki = jnp.asarray([p[1] for p in pairs], jnp.int32)
    full = lambda a: pl.BlockSpec(a.shape, lambda b, h, p, qi, ki: (0,) * a.ndim)
    grid_spec = pltpu.PrefetchScalarGridSpec(
        num_scalar_prefetch=2,
        grid=(nb, A_HEADS, len(pairs)),
        in_specs=[pl.BlockSpec((None, t, A_DV), lambda b, h, p, qi, ki: (b, qi[p], h)),
                  pl.BlockSpec((None, t, A_DV), lambda b, h, p, qi, ki: (b, ki[p], h)),
                  pl.BlockSpec((None, t, A_DV), lambda b, h, p, qi, ki: (b, ki[p], h)),
                  full(lamvec), full(gain)],
        out_specs=pl.BlockSpec((None, t, A_DV), lambda b, h, p, qi, ki: (b, qi[p], h)),
        scratch_shapes=[pltpu.VMEM((2, t, 1), F32), pltpu.VMEM((2, t, 1), F32),
                        pltpu.VMEM((2, t, A_DV), F32)],
    )
    return pl.pallas_call(
        functools.partial(_diff_kernel, lam_init=lam_init),
        out_shape=jax.ShapeDtypeStruct((nb, s, A_HEADS * A_DV), BF16),
        grid_spec=grid_spec,
        compiler_params=_cparams("parallel", "parallel", "arbitrary"),
        name="diff_attn_prompt",
    )(qi, ki, aq, ak, av, lamvec, gain)


def _sort_key(score):
    score = jnp.where(score == 0.0, 0.0, score)
    bits = lax.bitcast_convert_type(score, jnp.int32)
    return jnp.where(bits < 0, bits ^ jnp.int32(INT_MAX), bits)


def _kth_largest(count_ge, k, shape):
    def body(i, tu):
        cand_u = tu | lax.shift_left(jnp.int32(1), 31 - i)
        cnt = count_ge(cand_u ^ jnp.int32(INT_MIN))
        return jnp.where(cnt >= k, cand_u, tu)
    tu = lax.fori_loop(0, 32, body, jnp.zeros(shape, jnp.int32))
    return tu ^ jnp.int32(INT_MIN)


def _tie_cut(count_eq_below, need, nbits, shape):
    def body(i, j):
        cand = j | lax.shift_left(jnp.int32(1), nbits - 1 - i)
        return jnp.where(count_eq_below(cand) < need, cand, j)
    return lax.fori_loop(0, nbits, body, jnp.zeros(shape, jnp.int32))


def _dsa_kernel(iq_ref, misc_ref, ik2_ref, bq_ref, bk_ref, bv_ref, o_ref,
                key_ref, bias_ref, jstar_ref, m_ref, l_ref, acc_ref, *, tq, lc, topk, nbits):
    i = pl.program_id(1)
    q0 = i * tq
    n_chunks = (q0 + tq + lc - 1) // lc
    lane128 = lax.broadcasted_iota(jnp.int32, (tq, LANES), 1)
    low = lane128 < HEAD_DIM

    def head_block(ref, h):
        blk = ref[:, (h // 2) * LANES:(h // 2 + 1) * LANES]
        keep = low if h % 2 == 0 else jnp.logical_not(low)
        return jnp.where(keep, blk, jnp.zeros_like(blk))

    misc = misc_ref[...]
    iqh = [head_block(iq_ref, h) for h in range(IDX_HEADS)]
    wh = [misc[:, IW_LANE0 + h:IW_LANE0 + h + 1] for h in range(IDX_HEADS)]
    qpos = q0 + lax.broadcasted_iota(jnp.int32, (tq, lc), 0)
    col = lax.broadcasted_iota(jnp.int32, (tq, lc), 1)

    def score_chunk(c, carry):
        off = pl.multiple_of(c * lc, lc)
        ikc = ik2_ref[pl.ds(off, lc), :]
        sc = jnp.zeros((tq, lc), F32)
        for h in range(IDX_HEADS):
            sc = sc + jnp.maximum(_dot_nt(iqh[h], ikc), 0.0) * wh[h]
        key = _sort_key(sc * IDX_SCALE)
        key_ref[c] = jnp.where(off + col <= qpos, key, jnp.int32(INT_MIN))
        return carry

    lax.fori_loop(0, n_chunks, score_chunk, 0)

    def count_rows(pred):
        def body(c, acc):
            kc = key_ref[c]
            for sb in range(lc // LANES):
                acc = acc + jnp.where(pred(kc[:, sb * LANES:(sb + 1) * LANES], c * lc + sb * LANES),
                                      1, 0)
            return acc
        acc = lax.fori_loop(0, n_chunks, body, jnp.zeros((tq, LANES), jnp.int32))
        return jnp.sum(acc, axis=-1, keepdims=True)

    thr = _kth_largest(lambda cand: count_rows(lambda kc, _: kc >= cand), topk, (tq, 1))
    cnt_ge = count_rows(lambda kc, _: kc >= thr)
    straddle = jnp.logical_and(cnt_ge > topk, thr != jnp.int32(INT_MIN))
    jstar_ref[...] = jnp.full((tq, 1), INT_MAX, jnp.int32)

    @pl.when(jnp.max(jnp.where(straddle, 1, 0)) > 0)
    def _():
        need = topk - count_rows(lambda kc, _: kc > thr)

        def eq_below(cand):
            return count_rows(lambda kc, base: jnp.logical_and(kc == thr, base + lane128 < cand))

        jstar_ref[...] = _tie_cut(eq_below, need, nbits, (tq, 1))

    jstar = jstar_ref[...]

    def bias_chunk(c, carry):
        kc = key_ref[c]
        kpos = c * lc + col
        b = jnp.where(kc > thr, 0.0, jnp.where(kc == thr, jnp.where(kpos <= jstar, 0.0, NEG), NEG))
        bias_ref[c] = jnp.where(kc == jnp.int32(INT_MIN), NEG, b)
        return carry

    lax.fori_loop(0, n_chunks, bias_chunk, 0)

    outs = []
    for h in range(B_HEADS):
        qh = head_block(bq_ref, h)
        j = h // 2
        m_ref[...] = jnp.full((tq, 1), NEG, F32)
        l_ref[...] = jnp.zeros((tq, 1), F32)
        acc_ref[...] = jnp.zeros((tq, LANES), F32)

        def att_chunk(c, carry, qh=qh, j=j):
            off = pl.multiple_of(c * lc, lc)
            kc = bk_ref[pl.ds(off, lc), j * LANES:(j + 1) * LANES]
            vc = bv_ref[pl.ds(off, lc), j * LANES:(j + 1) * LANES]
            s = _dot_nt(qh, kc) + bias_ref[c]
            m_prev = m_ref[...]
            m_new = jnp.maximum(m_prev, jnp.max(s, axis=-1, keepdims=True))
            alpha = jnp.exp(m_prev - m_new)
            p = jnp.exp(s - m_new)
            l_ref[...] = alpha * l_ref[...] + jnp.sum(p, axis=-1, keepdims=True)
            acc_ref[...] = alpha * acc_ref[...] + _dot(p.astype(BF16), vc)
            m_ref[...] = m_new
            return carry

        lax.fori_loop(0, n_chunks, att_chunk, 0)
        outs.append(acc_ref[...] / l_ref[...])

    for j in range(B_HEADS // 2):
        o_ref[:, j * LANES:(j + 1) * LANES] = jnp.where(low, outs[2 * j], outs[2 * j + 1]).astype(o_ref.dtype)


def _dsa_prompt(iq, misc, ik2, bq, bk, bv):
    nb, s, _ = iq.shape
    tq = _pick_tile(s, 256, 16)
    lc = _pick_tile(s, 512, LANES)
    topk = min(TOPK_MAX, s // 4)
    nbits = max(1, (s - 1).bit_length())
    row = lambda w: pl.BlockSpec((None, tq, w), lambda b, i: (b, i, 0))
    seq = lambda w: pl.BlockSpec((None, s, w), lambda b, i: (b, 0, 0))
    return pl.pallas_call(
        functools.partial(_dsa_kernel, tq=tq, lc=lc, topk=topk, nbits=nbits),
        out_shape=jax.ShapeDtypeStruct((nb, s, B_HEADS * B_DH), BF16),
        grid=(nb, s // tq),
        in_specs=[row(SLAB), row(MISC), seq(MISC), row(SLAB), seq(SLAB), seq(SLAB)],
        out_specs=row(B_HEADS * B_DH),
        scratch_shapes=[pltpu.VMEM((s // lc, tq, lc), jnp.int32),
                        pltpu.VMEM((s // lc, tq, lc), F32),
                        pltpu.VMEM((tq, 1), jnp.int32),
                        pltpu.VMEM((tq, 1), F32), pltpu.VMEM((tq, 1), F32),
                        pltpu.VMEM((tq, LANES), F32)],
        compiler_params=_cparams("parallel", "arbitrary"),
        name="dsa_prompt",
    )(iq, misc, ik2, bq, bk, bv)


def _page_specs(n, shape, layer, pps):
    def mk(u):
        return pl.BlockSpec((None, None) + shape,
                            lambda b, g, pt, u=u: (layer, pt[b, g * pps + u], 0, 0))
    return [mk(u) for u in range(n)]


def _diff_sample_kernel(pt_ref, q_ref, kn_ref, vn_ref, lam_ref, gain_ref, *rest, pps, lam_init):
    k_refs = rest[:pps]
    v_refs = rest[pps:2 * pps]
    o_ref, m_ref, l_ref, acc_ref = rest[2 * pps:]
    g = pl.program_id(1)
    nmap = 2 * A_HEADS
    width = A_HEADS * A_DV

    @pl.when(g == 0)
    def _():
        m_ref[...] = jnp.full(m_ref.shape, NEG, F32)
        l_ref[...] = jnp.zeros(l_ref.shape, F32)
        acc_ref[...] = jnp.zeros(acc_ref.shape, F32)

    row = lax.broadcasted_iota(jnp.int32, (nmap, width), 0)
    lane = lax.broadcasted_iota(jnp.int32, (nmap, width), 1)
    qf = jnp.where(lane // A_DK == row, jnp.broadcast_to(q_ref[...].astype(F32), (nmap, width)), 0.0)
    qbd = qf.astype(BF16)

    s = jnp.concatenate([_dot_nt(qbd, k_refs[u][...].astype(BF16)) for u in range(pps)], axis=1)
    m_prev = m_ref[...]
    m_new = jnp.maximum(m_prev, jnp.max(s, axis=-1, keepdims=True))
    alpha = jnp.exp(m_prev - m_new)
    p = jnp.exp(s - m_new)
    l_ref[...] = alpha * l_ref[...] + jnp.sum(p, axis=-1, keepdims=True)
    pv = jnp.zeros((nmap, width), F32)
    for u in range(pps):
        pv = pv + _dot(p[:, u * LANES:(u + 1) * LANES].astype(BF16), v_refs[u][...].astype(BF16))
    acc_ref[...] = alpha * acc_ref[...] + pv
    m_ref[...] = m_new

    @pl.when(g == pl.num_programs(1) - 1)
    def _():
        s_self = jnp.sum(qf * kn_ref[...], axis=-1, keepdims=True)
        m_prev = m_ref[...]
        m_fin = jnp.maximum(m_prev, s_self)
        alpha = jnp.exp(m_prev - m_fin)
        p_self = jnp.exp(s_self - m_fin)
        l_fin = alpha * l_ref[...] + p_self
        o = (alpha * acc_ref[...] + p_self * vn_ref[...]) / l_fin
        head = lane // A_DV
        o0 = jnp.sum(jnp.where(row == 2 * head, o, 0.0), axis=0, keepdims=True)
        o1 = jnp.sum(jnp.where(row == 2 * head + 1, o, 0.0), axis=0, keepdims=True)
        d = o0 - _lambda(lam_ref, lam_init) * o1
        gain = gain_ref[...]
        for hd in range(A_HEADS):
            dh = d[:, hd * A_DV:(hd + 1) * A_DV]
            y = dh * lax.rsqrt(jnp.mean(dh * dh, axis=-1, keepdims=True) + EPS) * gain
            o_ref[:, hd * A_DV:(hd + 1) * A_DV] = (y * (1.0 - lam_init)).astype(o_ref.dtype)


def _diff_sample(page_table, aq, ak_new, av_new, lamvec, gain, cache_k, cache_v, layer, lam_init):
    ns, npg = page_table.shape
    pps = _pick_tile(npg, 8, 1)
    page = cache_k.shape[2]
    width = A_HEADS * A_DV
    per_b = lambda w: pl.BlockSpec((None, 1, w), lambda b, g, pt: (b, 0, 0))
    full = lambda a: pl.BlockSpec(a.shape, lambda b, g, pt: (0,) * a.ndim)
    grid_spec = pltpu.PrefetchScalarGridSpec(
        num_scalar_prefetch=1,
        grid=(ns, npg // pps),
        in_specs=[per_b(width), per_b(width), per_b(width), full(lamvec), full(gain)]
        + _page_specs(pps, (page, width), layer, pps) + _page_specs(pps, (page, width), layer, pps),
        out_specs=per_b(width),
        scratch_shapes=[pltpu.VMEM((2 * A_HEADS, 1), F32), pltpu.VMEM((2 * A_HEADS, 1), F32),
                        pltpu.VMEM((2 * A_HEADS, width), F32)],
    )
    return pl.pallas_call(
        functools.partial(_diff_sample_kernel, pps=pps, lam_init=lam_init),
        out_shape=jax.ShapeDtypeStruct((ns, 1, width), BF16),
        grid_spec=grid_spec,
        compiler_params=_cparams("parallel", "arbitrary"),
        name="diff_attn_sample",
    )(page_table, aq, ak_new, av_new, lamvec, gain, *([cache_k] * pps), *([cache_v] * pps))


def _idx_sample_kernel(pt_ref, iq_ref, iw_ref, *rest, pps):
    ik_refs = rest[:pps]
    o_ref = rest[pps]
    iq = iq_ref[...]
    w = iw_ref[...]
    rows = []
    for u in range(pps):
        lg = _dot_nt(iq, ik_refs[u][...].astype(BF16))
        rows.append(jnp.sum(jnp.maximum(lg, 0.0) * w, axis=0, keepdims=True) * IDX_SCALE)
    o_ref[...] = jnp.concatenate(rows, axis=0)


def _idx_sample(page_table, iq8, iw8, cache_ik, layer):
    ns, npg = page_table.shape
    pps = _pick_tile(npg, 8, 8)
    page = cache_ik.shape[2]
    grid_spec = pltpu.PrefetchScalarGridSpec(
        num_scalar_prefetch=1,
        grid=(ns, npg // pps),
        in_specs=[pl.BlockSpec((None, IDX_HEADS, IDX_DIM), lambda b, g, pt: (b, 0, 0)),
                  pl.BlockSpec((None, IDX_HEADS, 1), lambda b, g, pt: (b, 0, 0))]
        + _page_specs(pps, (page, IDX_DIM), layer, pps),
        out_specs=pl.BlockSpec((None, pps, page), lambda b, g, pt: (b, g, 0)),
    )
    return pl.pallas_call(
        functools.partial(_idx_sample_kernel, pps=pps),
        out_shape=jax.ShapeDtypeStruct((ns, npg, page), F32),
        grid_spec=grid_spec,
        compiler_params=_cparams("parallel", "arbitrary"),
        name="idx_scores_sample",
    )(page_table, iq8, iw8, *([cache_ik] * pps))


def _dsa_sample_kernel(pt_ref, sc_ref, iq_ref, iw_ref, ikn_ref, q_ref, kn_ref, vn_ref, *rest,
                       pps, topk, nbits, past):
    k_refs = rest[:pps]
    v_refs = rest[pps:2 * pps]
    o_ref, bias_ref, bself_ref, m_ref, l_ref, acc_ref = rest[2 * pps:]
    g = pl.program_id(1)
    width = B_HEADS * B_DH
    npg, page = sc_ref.shape

    @pl.when(g == 0)
    def _():
        m_ref[...] = jnp.full(m_ref.shape, NEG, F32)
        l_ref[...] = jnp.zeros(l_ref.shape, F32)
        acc_ref[...] = jnp.zeros(acc_ref.shape, F32)
        lg = jnp.sum(iq_ref[...].astype(F32) * ikn_ref[...].astype(BF16).astype(F32),
                     axis=-1, keepdims=True)
        s_self = jnp.sum(jnp.maximum(lg, 0.0) * iw_ref[...], axis=0, keepdims=True) * IDX_SCALE
        key_self = _sort_key(s_self)
        key = _sort_key(sc_ref[...])
        kpos = (lax.broadcasted_iota(jnp.int32, key.shape, 0) * page
                + lax.broadcasted_iota(jnp.int32, key.shape, 1))

        def total(mask, self_mask):
            c = jnp.sum(jnp.where(mask, 1, 0), axis=0, keepdims=True)
            return jnp.sum(c, axis=1, keepdims=True) + jnp.where(self_mask, 1, 0)

        thr = _kth_largest(lambda cand: total(key >= cand, key_self >= cand), topk, (1, 1))
        need = topk - total(key > thr, key_self > thr)
        jstar = _tie_cut(
            lambda cand: total(jnp.logical_and(key == thr, kpos < cand),
                               jnp.logical_and(key_self == thr, past < cand)),
            need, nbits, (1, 1))
        bias_ref[...] = jnp.where(
            key > thr, 0.0, jnp.where(key == thr, jnp.where(kpos <= jstar, 0.0, NEG), NEG))
        bself_ref[...] = jnp.where(
            key_self > thr, 0.0, jnp.where(key_self == thr, jnp.where(past <= jstar, 0.0, NEG), NEG))

    row = lax.broadcasted_iota(jnp.int32, (B_HEADS, width), 0)
    lane = lax.broadcasted_iota(jnp.int32, (B_HEADS, width), 1)
    qf = jnp.where(lane // B_DH == row, jnp.broadcast_to(q_ref[...].astype(F32), (B_HEADS, width)), 0.0)
    qbd = qf.astype(BF16)

    s = jnp.concatenate(
        [_dot_nt(qbd, k_refs[u][...].astype(BF16)) + bias_ref[pl.ds(g * pps + u, 1), :]
         for u in range(pps)], axis=1)
    m_prev = m_ref[...]
    m_new = jnp.maximum(m_prev, jnp.max(s, axis=-1, keepdims=True))
    alpha = jnp.exp(m_prev - m_new)
    p = jnp.exp(s - m_new)
    l_ref[...] = alpha * l_ref[...] + jnp.sum(p, axis=-1, keepdims=True)
    pv = jnp.zeros((B_HEADS, width), F32)
    for u in range(pps):
        pv = pv + _dot(p[:, u * LANES:(u + 1) * LANES].astype(BF16), v_refs[u][...].astype(BF16))
    acc_ref[...] = alpha * acc_ref[...] + pv
    m_ref[...] = m_new

    @pl.when(g == pl.num_programs(1) - 1)
    def _():
        s_self = jnp.sum(qf * kn_ref[...], axis=-1, keepdims=True) + bself_ref[...]
        m_prev = m_ref[...]
        m_fin = jnp.maximum(m_prev, s_self)
        alpha = jnp.exp(m_prev - m_fin)
        p_self = jnp.exp(s_self - m_fin)
        l_fin = alpha * l_ref[...] + p_self
        o = (alpha * acc_ref[...] + p_self * vn_ref[...]) / l_fin
        o_ref[...] = jnp.sum(jnp.where(lane // B_DH == row, o, 0.0), axis=0,
                             keepdims=True).astype(o_ref.dtype)


def _dsa_sample(page_table, scores, iq8, iw8, ik_new, bq, bk_new, bv_new, cache_k, cache_v, layer):
    ns, npg = page_table.shape
    pps = _pick_tile(npg, 8, 1)
    page = cache_k.shape[2]
    past = npg * page
    width = B_HEADS * B_DH
    topk = min(TOPK_MAX, (past + 1) // 4)
    nbits = max(1, past.bit_length())
    per_b = lambda r, w: pl.BlockSpec((None, r, w), lambda b, g, pt: (b, 0, 0))
    grid_spec = pltpu.PrefetchScalarGridSpec(
        num_scalar_prefetch=1,
        grid=(ns, npg // pps),
        in_specs=[per_b(npg, page), per_b(IDX_HEADS, IDX_DIM), per_b(IDX_HEADS, 1), per_b(1, IDX_DIM),
                  per_b(1, width), per_b(1, width), per_b(1, width)]
        + _page_specs(pps, (page, width), layer, pps) + _page_specs(pps, (page, width), layer, pps),
        out_specs=per_b(1, width),
        scratch_shapes=[pltpu.VMEM((npg, page), F32), pltpu.VMEM((1, 1), F32),
                        pltpu.VMEM((B_HEADS, 1), F32), pltpu.VMEM((B_HEADS, 1), F32),
                        pltpu.VMEM((B_HEADS, width), F32)],
    )
    return pl.pallas_call(
        functools.partial(_dsa_sample_kernel, pps=pps, topk=topk, nbits=nbits, past=past),
        out_shape=jax.ShapeDtypeStruct((ns, 1, width), BF16),
        grid_spec=grid_spec,
        compiler_params=_cparams("parallel", "arbitrary"),
        name="dsa_sample",
    )(page_table, scores, iq8, iw8, ik_new, bq, bk_new, bv_new,
      *([cache_k] * pps), *([cache_v] * pps))


def _outproj_kernel(*refs, route):
    if route:
        (ao_ref, bo_ref, w_ref, x_ref, g1_ref, sc_ref, sh_ref, gf_ref, wr_ref, br_ref,
         x1_ref, h2_ref, comb_ref) = refs
    else:
        ao_ref, bo_ref, w_ref, x_ref, g1_ref, sc_ref, sh_ref, gf_ref, x1_ref, h2_ref = refs
    wa = ao_ref.shape[-1]
    mix = _dot(ao_ref[...], w_ref[:wa, :]) + _dot(bo_ref[...], w_ref[wa:, :])
    x1 = x_ref[...] + g1_ref[...] * mix
    x1_ref[...] = x1
    h = x1 * lax.rsqrt(jnp.mean(x1 * x1, axis=-1, keepdims=True) + EPS) * gf_ref[...]
    h = h * (1.0 + sc_ref[...]) + sh_ref[...]
    h2_ref[...] = h.astype(BF16)
    if route:
        logits = jnp.dot(h, wr_ref[...], precision=HIGHEST, preferred_element_type=F32) + br_ref[...]
        lane = lax.broadcasted_iota(jnp.int32, logits.shape, 1)
        logits = jnp.where(lane < N_EXPERTS, logits, -jnp.inf)
        v1 = jnp.max(logits, axis=-1, keepdims=True)
        i1 = jnp.min(jnp.where(logits == v1, lane, LANES), axis=-1, keepdims=True)
        rest = jnp.where(lane == i1, -jnp.inf, logits)
        v2 = jnp.max(rest, axis=-1, keepdims=True)
        i2 = jnp.min(jnp.where(rest == v2, lane, LANES), axis=-1, keepdims=True)
        e2 = jnp.exp(v2 - v1)
        den = 1.0 + e2
        comb_ref[...] = jnp.where(lane == i1, 1.0 / den, jnp.where(lane == i2, e2 / den, 0.0))


def _outproj(ao, bo, w_bf, x3, gate1, scale2, shift2, gf, router=None):
    nb, s, d = x3.shape
    tm = _pick_tile(s, 512, 16)
    row = lambda w: pl.BlockSpec((None, tm, w), lambda b, i: (b, i, 0))
    full = lambda a: pl.BlockSpec(a.shape, lambda b, i: (0,) * a.ndim)
    ins = [ao, bo, w_bf, x3, gate1, scale2, shift2, gf]
    in_specs = [row(ao.shape[-1]), row(bo.shape[-1]), full(w_bf), row(d), _mod_spec(gate1, tm),
                _mod_spec(scale2, tm), _mod_spec(shift2, tm), full(gf)]
    out_shape = [jax.ShapeDtypeStruct((nb, s, d), F32), jax.ShapeDtypeStruct((nb, s, d), BF16)]
    out_specs = [row(d), row(d)]
    if router is not None:
        ins += list(router)
        in_specs += [full(router[0]), full(router[1])]
        out_shape.append(jax.ShapeDtypeStruct((nb, s, LANES), F32))
        out_specs.append(row(LANES))
    return pl.pallas_call(
        functools.partial(_outproj_kernel, route=router is not None),
        out_shape=tuple(out_shape),
        grid=(nb, s // tm),
        in_specs=in_specs,
        out_specs=tuple(out_specs),
        compiler_params=_cparams("parallel", "parallel"),
        name="out_proj",
    )(*ins)


def _ffn_kernel(h_ref, w1_ref, w3_ref, w2_ref, x_ref, g2_ref, o_ref, acc_ref):
    j = pl.program_id(2)

    @pl.when(j == 0)
    def _():
        acc_ref[...] = jnp.zeros(acc_ref.shape, F32)

    h = h_ref[...]
    t = jax.nn.silu(_dot(h, w1_ref[...])) * _dot(h, w3_ref[...])
    acc_ref[...] += _dot(t.astype(BF16), w2_ref[...])

    @pl.when(j == pl.num_programs(2) - 1)
    def _():
        o_ref[...] = x_ref[...] + g2_ref[...] * acc_ref[...]


def _ffn_dense(h2, w1, w3, w2, x1, gate2):
    nb, s, d = x1.shape
    ff = w1.shape[1]
    tm = _pick_tile(s, 512, 16)
    tf = _pick_tile(ff, 1408, LANES)
    row = pl.BlockSpec((None, tm, d), lambda b, i, j: (b, i, 0))
    return pl.pallas_call(
        _ffn_kernel,
        out_shape=jax.ShapeDtypeStruct((nb, s, d), F32),
        grid=(nb, s // tm, ff // tf),
        in_specs=[row, pl.BlockSpec((d, tf), lambda b, i, j: (0, j)),
                  pl.BlockSpec((d, tf), lambda b, i, j: (0, j)),
                  pl.BlockSpec((tf, d), lambda b, i, j: (j, 0)), row, _mod_spec(gate2, tm)],
        out_specs=row,
        scratch_shapes=[pltpu.VMEM((tm, d), F32)],
        compiler_params=_cparams("parallel", "parallel", "arbitrary"),
        name="ffn_dense",
    )(h2, w1, w3, w2, x1, gate2)


def _moe_kernel(h_ref, comb_ref, w1_ref, w3_ref, w2_ref, x_ref, g2_ref, o_ref, acc_ref):
    e = pl.program_id(2)
    j = pl.program_id(3)

    @pl.when(jnp.logical_and(e == 0, j == 0))
    def _():
        acc_ref[...] = jnp.zeros(acc_ref.shape, F32)

    comb = comb_ref[...]
    lane = lax.broadcasted_iota(jnp.int32, comb.shape, 1)
    ce = jnp.sum(jnp.where(lane == e, comb, 0.0), axis=-1, keepdims=True)
    h = h_ref[...]
    t = jax.nn.silu(_dot(h, w1_ref[...])) * _dot(h, w3_ref[...])
    acc_ref[...] += ce * _dot(t.astype(BF16), w2_ref[...])

    @pl.when(jnp.logical_and(e == pl.num_programs(2) - 1, j == pl.num_programs(3) - 1))
    def _():
        o_ref[...] = x_ref[...] + g2_ref[...] * acc_ref[...]


def _ffn_moe(h2, comb, w1, w3, w2, x1, gate2):
    nb, s, d = x1.shape
    ne, _, ff = w1.shape
    tm = _pick_tile(s, 512, 16)
    tf = _pick_tile(ff, 1792, LANES)
    row = lambda w: pl.BlockSpec((None, tm, w), lambda b, i, e, j: (b, i, 0))
    return pl.pallas_call(
        _moe_kernel,
        out_shape=jax.ShapeDtypeStruct((nb, s, d), F32),
        grid=(nb, s // tm, ne, ff // tf),
        in_specs=[row(d), row(LANES),
                  pl.BlockSpec((None, d, tf), lambda b, i, e, j: (e, 0, j)),
                  pl.BlockSpec((None, d, tf), lambda b, i, e, j: (e, 0, j)),
                  pl.BlockSpec((None, tf, d), lambda b, i, e, j: (e, j, 0)), row(d),
                  _mod_spec(gate2, tm)],
        out_specs=row(d),
        scratch_shapes=[pltpu.VMEM((tm, d), F32)],
        compiler_params=_cparams("parallel", "parallel", "arbitrary", "arbitrary"),
        name="ffn_moe",
    )(h2, comb, w1, w3, w2, x1, gate2)


def kernel(x_prompt, x_sample, cache_a_k, cache_a_v, cache_b_k, cache_b_v, cache_b_ik, page_table, c_prompt, c_sample, w_ada, b_ada, g_mix, g_ffn, w_in, w_out, a_q_norm, a_k_norm, b_q_norm, b_k_norm, lam_q1, lam_k1, lam_q2, lam_k2, a_sub_norm, w1_dense, w3_dense, w2_dense, w_router, b_router, w1_exp, w3_exp, w2_exp):
    nb, s, d = x_prompt.shape
    ns, dec_seq, _ = x_sample.shape
    assert dec_seq == 1
    depth = w_in.shape[0]
    n_pool, page = cache_a_k.shape[1:3]
    past = page_table.shape[1] * page
    assert w_in.shape[2] + (NPROJ - w_in.shape[2]) == NPROJ and w_in.shape[2] == N_SLABS * SLAB + IDX_DIM + IDX_HEADS

    rows = nb + ns
    rpad = -rows % 8
    c_all = jnp.concatenate([c_prompt, c_sample, jnp.zeros((rpad, d), F32)], axis=0)
    mod = _ada(c_all, w_ada, b_ada)

    tabs_p = _rope_tables(jnp.arange(s, dtype=jnp.int32))
    tabs_s = _rope_tables(jnp.full((ns,), past, jnp.int32))
    gidx = jnp.arange(SLAB) // HEAD_DIM
    m64 = (gidx[:, None] == gidx[None, :]).astype(BF16) * (1.0 / HEAD_DIM)

    ca_k = cache_a_k.reshape(depth, n_pool, page, A_HEADS * 2 * A_DK)
    ca_v = cache_a_v.reshape(depth, n_pool, page, A_HEADS * A_DV)
    cb_k = cache_b_k.reshape(depth, n_pool, page, B_HEADS * B_DH)
    cb_v = cache_b_v.reshape(depth, n_pool, page, B_HEADS * B_DH)

    xp = x_prompt
    xs = x_sample.reshape(1, ns, d)
    outs_p, outs_s = [], []
    for l in range(depth):
        lam_init = 0.8 - 0.6 * math.exp(-0.3 * l)
        w_in_bf = jnp.pad(w_in[l], ((0, 0), (0, NPROJ - w_in.shape[2]))).astype(BF16)
        w_out_bf = w_out[l].astype(BF16)
        rep = SLAB // HEAD_DIM
        gains = jnp.stack([jnp.tile(a_q_norm[l], rep), jnp.tile(a_k_norm[l], rep),
                           jnp.tile(b_q_norm[l], rep), jnp.tile(b_k_norm[l], rep)])
        lamvec = jnp.stack([lam_q1[l], lam_k1[l], lam_q2[l], lam_k2[l]])
        sub_gain = a_sub_norm[l].reshape(1, A_DV)
        g_mix_l = g_mix[l].reshape(1, d)
        g_ffn_l = g_ffn[l].reshape(1, d)
        mods_p = [m.reshape(nb, 1, d) for m in jnp.split(mod[l, :nb], 6, axis=-1)]
        mods_s = [m.reshape(1, ns, d) for m in jnp.split(mod[l, nb:rows], 6, axis=-1)]
        moe = l % 2 == 1
        jj = l // 2
        if moe:
            router = (jnp.pad(w_router[jj], ((0, 0), (0, LANES - N_EXPERTS))),
                      jnp.pad(b_router[jj], (0, LANES - N_EXPERTS)).reshape(1, LANES))
            ffw = (w1_exp[jj].astype(BF16), w3_exp[jj].astype(BF16), w2_exp[jj].astype(BF16))
        else:
            router = None
            ffw = (w1_dense[jj].astype(BF16), w3_dense[jj].astype(BF16), w2_dense[jj].astype(BF16))

        def mix_and_ffn(x3, mods, ao, bo):
            shift1, scale1, gate1, shift2, scale2, gate2 = mods
            res = _outproj(ao, bo, w_out_bf, x3, gate1, scale2, shift2, g_ffn_l, router)
            if moe:
                x1, h2, comb = res
                return _ffn_moe(h2, comb, *ffw, x1, gate2)
            x1, h2 = res
            return _ffn_dense(h2, *ffw, x1, gate2)

        shift1, scale1 = mods_p[0], mods_p[1]
        (akf, avf, bkf, bvf, miscf, aqb, akb, avb, bqb, bkb, bvb, iqb, ik2b) = _project(
            xp, scale1, shift1, g_mix_l, w_in_bf, gains, tabs_p, m64)
        ao = _diff_prompt(aqb, akb, avb, lamvec, sub_gain, lam_init)
        bo = _dsa_prompt(iqb, miscf, ik2b, bqb, bkb, bvb)
        xp = mix_and_ffn(xp, mods_p, ao, bo)
        outs_p.append((akf.reshape(nb, s, A_HEADS, 2, A_DK), avf.reshape(nb, s, A_HEADS, A_DV),
                       bkf.reshape(nb, s, B_HEADS, B_DH), bvf.reshape(nb, s, B_HEADS, B_DH),
                       miscf[..., :IDX_DIM]))

        shift1, scale1 = mods_s[0], mods_s[1]
        (akf, avf, bkf, bvf, miscf, aqb, akb, avb, bqb, bkb, bvb, iqb, ik2b) = _project(
            xs, scale1, shift1, g_mix_l, w_in_bf, gains, tabs_s, m64)
        per_tok = lambda a: a.reshape(ns, 1, a.shape[-1])
        ao = _diff_sample(page_table, per_tok(aqb), per_tok(akf), per_tok(avf), lamvec, sub_gain,
                          ca_k, ca_v, l, lam_init)
        iq8 = iqb.reshape(ns, IDX_HEADS, IDX_DIM)
        iw8 = miscf[0, :, IW_LANE0:IW_LANE0 + IDX_HEADS].reshape(ns, IDX_HEADS, 1)
        ik_new = miscf[0, :, :IDX_DIM].reshape(ns, 1, IDX_DIM)
        scores = _idx_sample(page_table, iq8, iw8, cache_b_ik, l)
        bo = _dsa_sample(page_table, scores, iq8, iw8, ik_new, per_tok(bqb), per_tok(bkf),
                         per_tok(bvf), cb_k, cb_v, l)
        xs = mix_and_ffn(xs, mods_s, ao.reshape(1, ns, -1), bo.reshape(1, ns, -1))
        outs_s.append((akf.reshape(ns, 1, A_HEADS, 2, A_DK), avf.reshape(ns, 1, A_HEADS, A_DV),
                       bkf.reshape(ns, 1, B_HEADS, B_DH), bvf.reshape(ns, 1, B_HEADS, B_DH),
                       miscf[0, :, :IDX_DIM].reshape(ns, 1, IDX_DIM)))

    stack = lambda rows_, i: jnp.stack([r[i] for r in rows_])
    return (xp, xs.reshape(ns, 1, d),
            stack(outs_p, 0), stack(outs_p, 1), stack(outs_p, 2), stack(outs_p, 3), stack(outs_p, 4),
            stack(outs_s, 0), stack(outs_s, 1), stack(outs_s, 2), stack(outs_s, 3), stack(outs_s, 4))
```

```python
import functools
import math

import jax
import jax.numpy as jnp
from jax import lax
from jax.experimental import pallas as pl
from jax.experimental.pallas import tpu as pltpu

F32 = jnp.float32
BF16 = jnp.bfloat16
HIGHEST = lax.Precision.HIGHEST

A_HEADS = 4
A_DK = 64
A_DV = 2 * A_DK
B_HEADS = 8
B_DH = 64
IDX_HEADS = 8
IDX_DIM = 64
IDX_SCALE = (IDX_HEADS ** -0.5) * (IDX_DIM ** -0.5)
TOPK_MAX = 256
ROPE_THETA = 500000.0
ROPE_FRACTION = 4
N_EXPERTS = 8
EPS = 1e-6

HEAD_DIM = 64
ROPE_HALF = HEAD_DIM // ROPE_FRACTION // 2
SLAB = 512
N_SLABS = 7
MISC = 128
IW_LANE0 = IDX_DIM
NPROJ = N_SLABS * SLAB + MISC
LANES = 128
INT_MIN = -2 ** 31
INT_MAX = 2 ** 31 - 1
NEG = -1e30
VMEM_LIMIT = 56 * 1024 * 1024


def _dot(a, b):
    return jnp.dot(a, b, preferred_element_type=F32)


def _dot_nt(a, b):
    return lax.dot_general(a, b, (((1,), (1,)), ((), ())), preferred_element_type=F32)


def _cparams(*sem):
    return pltpu.CompilerParams(dimension_semantics=sem, vmem_limit_bytes=VMEM_LIMIT)


def _col_reduce(x, reduce_fn, combine_fn, parts=8):
    n = x.shape[0]
    parts = parts if n % (8 * parts) == 0 else 1
    step = n // parts
    vals = [reduce_fn(x[k * step:(k + 1) * step], axis=0, keepdims=True) for k in range(parts)]
    while len(vals) > 1:
        vals = [combine_fn(vals[k], vals[k + 1]) for k in range(0, len(vals), 2)]
    return vals[0]


def _col_max(x):
    return _col_reduce(x, jnp.max, jnp.maximum)


def _col_sum(x):
    return _col_reduce(x, jnp.sum, jnp.add)


def _pick_tile(n, target, mult):
    best = None
    for t in range(mult, min(n, target) + 1, mult):
        if n % t == 0:
            best = t
    return best if best is not None else n


def _ada_kernel(c_ref, w_ref, b_ref, o_ref):
    c = c_ref[...]
    o_ref[...] = jnp.dot(jax.nn.silu(c), w_ref[...], precision=HIGHEST,
                         preferred_element_type=F32) + b_ref[...]


def _ada(c_all, w_ada, b_ada):
    depth, d, n = w_ada.shape
    r = c_all.shape[0]
    tn = _pick_tile(n, 1536, LANES)
    return pl.pallas_call(
        _ada_kernel,
        out_shape=jax.ShapeDtypeStruct((depth, r, n), F32),
        grid=(depth, n // tn),
        in_specs=[pl.BlockSpec((r, d), lambda l, j: (0, 0)),
                  pl.BlockSpec((None, d, tn), lambda l, j: (l, 0, j)),
                  pl.BlockSpec((None, 1, tn), lambda l, j: (l, 0, j))],
        out_specs=pl.BlockSpec((None, r, tn), lambda l, j: (l, 0, j)),
        compiler_params=_cparams("parallel", "parallel"),
        name="adaln",
    )(c_all, w_ada, b_ada.reshape(depth, 1, n))


def _rope_tables(pos):
    r = HEAD_DIM // ROPE_FRACTION
    inv = ROPE_THETA ** (-jnp.arange(ROPE_HALF, dtype=F32) * 2.0 / r)
    ang = pos.astype(F32)[:, None] * inv[None, :]
    cos, sin = jnp.cos(ang), jnp.sin(ang)
    s = pos.shape[0]
    rest = HEAD_DIM - 2 * ROPE_HALF
    z8 = jnp.zeros((s, ROPE_HALF), F32)
    zr = jnp.zeros((s, rest), F32)
    c = jnp.concatenate([cos, cos, jnp.ones((s, rest), F32)], axis=1)
    s1 = jnp.concatenate([-sin, z8, zr], axis=1)
    s2 = jnp.concatenate([z8, sin, zr], axis=1)
    rep = LANES // HEAD_DIM
    return tuple(jnp.concatenate([t] * rep, axis=1) for t in (c, s1, s2))


def _proj_kernel(x_ref, sc_ref, sh_ref, g_ref, w_ref, gains_ref, rc_ref, rs1_ref, rs2_ref,
                 m64_ref, *rest, prompt):
    if prompt:
        akf, avf, bkf, bvf, miscf, akb, bkb, ik2b, aqt, avt, bqt, bvt, iqt, iwt, h_scr = rest
    else:
        akf, avf, bkf, bvf, miscf, aqb, bqb, iqb, h_scr = rest
    x = x_ref[...]
    h = x * lax.rsqrt(jnp.mean(x * x, axis=-1, keepdims=True) + EPS) * g_ref[...]
    h = h * (1.0 + sc_ref[...]) + sh_ref[...]
    h_scr[...] = h.astype(BF16)

    c1, s1, s2 = rc_ref[...], rs1_ref[...], rs2_ref[...]
    rep = SLAB // LANES
    c4 = jnp.concatenate([c1] * rep, axis=1)
    s14 = jnp.concatenate([s1] * rep, axis=1)
    s24 = jnp.concatenate([s2] * rep, axis=1)
    m64 = m64_ref[...]

    def slab(i, width=SLAB):
        return _dot(h_scr[...], w_ref[:, i * SLAB:i * SLAB + width])

    def qknorm(p, gi):
        sq = p * p
        hi = sq.astype(BF16)
        lo = (sq - hi.astype(F32)).astype(BF16)
        ms = _dot(hi, m64) + _dot(lo, m64)
        return p * lax.rsqrt(ms + EPS) * gains_ref[gi:gi + 1, :]

    def rope(y, c, a, b):
        n = y.shape[-1]
        return y * c + pltpu.roll(y, n - ROPE_HALF, 1) * a + pltpu.roll(y, ROPE_HALF, 1) * b

    def put(val, row_ref=None, t_ref=None):
        if row_ref is not None:
            row_ref[...] = val.astype(row_ref.dtype)
        if t_ref is not None:
            t_ref[...] = val.T.astype(t_ref.dtype)

    aq = rope(qknorm(slab(0), 0), c4, s14, s24) * (A_DK ** -0.5)
    put(aq, None, aqt) if prompt else put(aq, aqb)
    ak = rope(qknorm(slab(1), 1), c4, s14, s24)
    akf[...] = ak
    if prompt:
        put(ak, akb)
    av = slab(2)
    avf[...] = av
    if prompt:
        put(av, None, avt)
    bq = rope(qknorm(slab(3), 2), c4, s14, s24) * (B_DH ** -0.5)
    put(bq, None, bqt) if prompt else put(bq, bqb)
    bk = rope(qknorm(slab(4), 3), c4, s14, s24)
    bkf[...] = bk
    if prompt:
        put(bk, bkb)
    bv = slab(5)
    bvf[...] = bv
    if prompt:
        put(bv, None, bvt)
    iq = rope(slab(6), c4, s14, s24)
    put(iq, None, iqt) if prompt else put(iq, iqb)
    pm = slab(N_SLABS, MISC)
    lane = lax.broadcasted_iota(jnp.int32, pm.shape, 1)
    misc = jnp.where(lane < IDX_DIM, rope(pm, c1, s1, s2), pm)
    miscf[...] = misc
    if prompt:
        ik2b[...] = jnp.where(lane < IDX_DIM, misc, pltpu.roll(misc, IDX_DIM, 1)).astype(BF16)
        iwt[...] = misc.T[IW_LANE0:IW_LANE0 + IDX_HEADS, :]


def _mod_spec(arr, tm):
    _, r, d = arr.shape
    if r == 1:
        return pl.BlockSpec((None, 1, d), lambda b, i, *_: (b, 0, 0))
    return pl.BlockSpec((None, tm, d), lambda b, i, *_: (b, i, 0))


def _proj_tile(s):
    return _pick_tile(s, 512, LANES if s % LANES == 0 else 16)


def _project(x3, scale, shift, g, w_bf, gains, tabs, m64, prompt):
    nb, s, d = x3.shape
    tm = _proj_tile(s)
    rc, rs1, rs2 = tabs
    row = lambda w: pl.BlockSpec((None, tm, w), lambda b, i: (b, i, 0))
    col = lambda r: pl.BlockSpec((None, r, tm), lambda b, i: (b, 0, i))
    tab = pl.BlockSpec((tm, LANES), lambda b, i: (i, 0))
    full = lambda a: pl.BlockSpec(a.shape, lambda b, i: (0,) * a.ndim)
    rows = lambda w, dt: (jax.ShapeDtypeStruct((nb, s, w), dt), row(w))
    cols = lambda r, dt: (jax.ShapeDtypeStruct((nb, r, s), dt), col(r))
    tiled = (jax.ShapeDtypeStruct((nb, s // tm, SLAB, tm), BF16),
             pl.BlockSpec((None, None, SLAB, tm), lambda b, i: (b, i, 0, 0)))
    outs = [rows(SLAB, F32)] * 4 + [rows(MISC, F32)]
    if prompt:
        outs += ([rows(SLAB, BF16)] * 2 + [rows(MISC, BF16)] + [cols(SLAB, BF16)] * 3 + [tiled]
                 + [cols(SLAB, BF16), cols(IDX_HEADS, F32)])
    else:
        outs += [rows(SLAB, BF16)] * 3
    return pl.pallas_call(
        functools.partial(_proj_kernel, prompt=prompt),
        out_shape=tuple(o[0] for o in outs),
        grid=(nb, s // tm),
        in_specs=[row(d), _mod_spec(scale, tm), _mod_spec(shift, tm), full(g), full(w_bf),
                  full(gains), tab, tab, tab, full(m64)],
        out_specs=tuple(o[1] for o in outs),
        scratch_shapes=[pltpu.VMEM((tm, d), BF16)],
        compiler_params=_cparams("parallel", "parallel"),
        name="in_proj",
    )(x3, scale, shift, g, w_bf, gains, rc, rs1, rs2, m64)


def _lambda(lam_ref, lam_init):
    lv = lam_ref[...]
    a = jnp.sum(lv[0:1] * lv[1:2], axis=-1, keepdims=True)
    b = jnp.sum(lv[2:3] * lv[3:4], axis=-1, keepdims=True)
    return jnp.exp(a) - jnp.exp(b) + lam_init


def _diff_kernel(qi_ref, ki_ref, qt_ref, k_ref, vt_ref, lam_ref, gain_ref, o_ref,
                 m_ref, l_ref, acc_ref, *, lam_init):
    p_id = pl.program_id(2)
    qi = qi_ref[p_id]
    ki = ki_ref[p_id]

    @pl.when(ki == 0)
    def _():
        m_ref[...] = jnp.full(m_ref.shape, NEG, F32)
        l_ref[...] = jnp.zeros(l_ref.shape, F32)
        acc_ref[...] = jnp.zeros(acc_ref.shape, F32)

    qt = qt_ref[...]
    row = lax.broadcasted_iota(jnp.int32, qt.shape, 0)
    zero = jnp.zeros_like(qt)
    qs = (jnp.where(row < A_DK, qt, zero), jnp.where(row >= A_DK, qt, zero))
    k = k_ref[...]
    vt = vt_ref[...]

    def update(masked):
        ss = [_dot(k, qs[mp]) for mp in range(2)]
        ps, alphas = [], []
        for mp in range(2):
            s = ss[mp]
            if masked:
                kpos = lax.broadcasted_iota(jnp.int32, s.shape, 0)
                qpos = lax.broadcasted_iota(jnp.int32, s.shape, 1)
                s = jnp.where(kpos <= qpos, s, NEG)
            m_prev = m_ref[mp]
            m_new = jnp.maximum(m_prev, _col_max(s))
            alpha = jnp.exp(m_prev - m_new)
            p = jnp.exp(s - m_new)
            l_ref[mp] = alpha * l_ref[mp] + _col_sum(p)
            m_ref[mp] = m_new
            ps.append(p.astype(BF16))
            alphas.append(alpha)
        for mp in range(2):
            acc_ref[mp] = alphas[mp] * acc_ref[mp] + _dot(vt, ps[mp])

    @pl.when(ki < qi)
    def _():
        update(False)

    @pl.when(ki == qi)
    def _():
        update(True)
        lam = _lambda(lam_ref, lam_init)
        o = acc_ref[0] / l_ref[0] - lam * (acc_ref[1] / l_ref[1])
        y = o * lax.rsqrt(jnp.mean(o * o, axis=0, keepdims=True) + EPS) * gain_ref[...]
        o_ref[...] = (y * (1.0 - lam_init)).T.astype(o_ref.dtype)


def _diff_prompt(aqt, ak, avt, lamvec, gain_col, lam_init):
    nb, s, _ = ak.shape
    t = _pick_tile(s, 512, LANES)
    nq = s // t
    pairs = [(i, j) for i in range(nq) for j in range(i + 1)]
    qi = jnp.asarray([p[0] for p in pairs], jnp.int32)
    ki = jnp.asarray([p[1] for p in pairs], jnp.int32)
    full = lambda a: pl.BlockSpec(a.shape, lambda b, h, p, qi, ki: (0,) * a.ndim)
    grid_spec = pltpu.PrefetchScalarGridSpec(
        num_scalar_prefetch=2,
        grid=(nb, A_HEADS, len(pairs)),
        in_specs=[pl.BlockSpec((None, A_DV, t), lambda b, h, p, qi, ki: (b, h, qi[p])),
                  pl.BlockSpec((None, t, A_DV), lambda b, h, p, qi, ki: (b, ki[p], h)),
                  pl.BlockSpec((None, A_DV, t), lambda b, h, p, qi, ki: (b, h, ki[p])),
                  full(lamvec), full(gain_col)],
        out_specs=pl.BlockSpec((None, t, A_DV), lambda b, h, p, qi, ki: (b, qi[p], h)),
        scratch_shapes=[pltpu.VMEM((2, 1, t), F32), pltpu.VMEM((2, 1, t), F32),
                        pltpu.VMEM((2, A_DV, t), F32)],
    )
    return pl.pallas_call(
        functools.partial(_diff_kernel, lam_init=lam_init),
        out_shape=jax.ShapeDtypeStruct((nb, s, A_HEADS * A_DV), BF16),
        grid_spec=grid_spec,
        compiler_params=_cparams("parallel", "parallel", "arbitrary"),
        name="diff_attn_prompt",
    )(qi, ki, aqt, ak, avt, lamvec, gain_col)


def _sort_key(score):
    score = jnp.where(score == 0.0, 0.0, score)
    bits = lax.bitcast_convert_type(score, jnp.int32)
    return jnp.where(bits < 0, bits ^ jnp.int32(INT_MAX), bits)


def _kth_largest(count_ge, k, shape, settled=None):
    def pending(cnt):
        p = cnt != k
        return p if settled is None else jnp.logical_and(p, jnp.logical_not(settled))

    def cond(st):
        i, _, cnt = st
        return jnp.logical_and(i < 32, jnp.max(jnp.where(pending(cnt), 1, 0)) > 0)

    def body(st):
        i, tu, cnt = st
        cand_u = tu | lax.shift_left(jnp.int32(1), 31 - i)
        c = count_ge(cand_u ^ jnp.int32(INT_MIN))
        take = c >= k
        return i + 1, jnp.where(take, cand_u, tu), jnp.where(take, c, cnt)

    _, tu, _ = lax.while_loop(
        cond, body, (jnp.int32(0), jnp.zeros(shape, jnp.int32), jnp.full(shape, -1, jnp.int32)))
    return tu ^ jnp.int32(INT_MIN)


def _tie_cut(count_eq_below, need, nbits, shape):
    def body(i, j):
        cand = j | lax.shift_left(jnp.int32(1), nbits - 1 - i)
        return jnp.where(count_eq_below(cand) < need, cand, j)
    return lax.fori_loop(0, nbits, body, jnp.zeros(shape, jnp.int32))


def _dsa_kernel(iqt_ref, iwt_ref, ik2_ref, bqt_ref, bk_ref, bvt_ref, o_ref,
                key_ref, bias_ref, jstar_ref, m_ref, l_ref, acc_ref, *, tq, lc, topk, nbits):
    i = pl.program_id(1)
    q0 = i * tq
    n_chunks = (q0 + tq + lc - 1) // lc
    row128 = lax.broadcasted_iota(jnp.int32, (LANES, tq), 0)
    low = row128 < HEAD_DIM

    def head_block(ref, h):
        blk = ref[(h // 2) * LANES:(h // 2 + 1) * LANES, :]
        keep = low if h % 2 == 0 else jnp.logical_not(low)
        return jnp.where(keep, blk, jnp.zeros_like(blk))

    kpos0 = lax.broadcasted_iota(jnp.int32, (lc, tq), 0)
    qpos = q0 + lax.broadcasted_iota(jnp.int32, (lc, tq), 1)

    iw = iwt_ref[...]
    iqh = [head_block(iqt_ref, h) for h in range(IDX_HEADS)]

    def score_chunk(c, carry):
        off = pl.multiple_of(c * lc, lc)
        ikc = ik2_ref[pl.ds(off, lc), :]
        sc = jnp.zeros((lc, tq), F32)
        for h in range(IDX_HEADS):
            sc = sc + jnp.maximum(_dot(ikc, iqh[h]), 0.0) * iw[h:h + 1, :]
        key = _sort_key(sc * IDX_SCALE)
        key_ref[c] = jnp.where(off + kpos0 <= qpos, key, jnp.int32(INT_MIN))
        return carry

    lax.fori_loop(0, n_chunks, score_chunk, 0)

    def count(pred):
        def body(c, acc):
            return acc + _col_sum(jnp.where(pred(key_ref[c], c * lc), 1, 0))
        return lax.fori_loop(0, n_chunks, body, jnp.zeros((1, tq), jnp.int32))

    few = q0 + lax.broadcasted_iota(jnp.int32, (1, tq), 1) < topk
    thr = _kth_largest(lambda cand: count(lambda kc, _: kc >= cand), topk, (1, tq), few)
    cnt_ge = count(lambda kc, _: kc >= thr)
    straddle = jnp.logical_and(cnt_ge > topk, thr != jnp.int32(INT_MIN))
    jstar_ref[...] = jnp.full((1, tq), INT_MAX, jnp.int32)

    @pl.when(jnp.max(jnp.where(straddle, 1, 0)) > 0)
    def _():
        need = topk - count(lambda kc, _: kc > thr)

        def eq_below(cand):
            return count(lambda kc, base: jnp.logical_and(kc == thr, base + kpos0 < cand))

        jstar_ref[...] = _tie_cut(eq_below, need, nbits, (1, tq))

    jstar = jstar_ref[...]

    def bias_chunk(c, carry):
        kc = key_ref[c]
        kpos = c * lc + kpos0
        b = jnp.where(kc > thr, 0.0, jnp.where(kc == thr, jnp.where(kpos <= jstar, 0.0, NEG), NEG))
        bias_ref[c] = jnp.where(kc == jnp.int32(INT_MIN), NEG, b)
        return carry

    lax.fori_loop(0, n_chunks, bias_chunk, 0)

    m_ref[...] = jnp.full(m_ref.shape, NEG, F32)
    l_ref[...] = jnp.zeros(l_ref.shape, F32)
    acc_ref[...] = jnp.zeros(acc_ref.shape, F32)
    bqh = [head_block(bqt_ref, h) for h in range(B_HEADS)]

    def att_chunk(c, carry):
        off = pl.multiple_of(c * lc, lc)
        bias = bias_ref[c]
        kcs = [bk_ref[pl.ds(off, lc), j * LANES:(j + 1) * LANES] for j in range(B_HEADS // 2)]
        ss = [_dot(kcs[h // 2], bqh[h]) for h in range(B_HEADS)]
        ps, alphas = [], []
        for h in range(B_HEADS):
            s = ss[h] + bias
            m_prev = m_ref[h]
            m_new = jnp.maximum(m_prev, _col_max(s))
            alpha = jnp.exp(m_prev - m_new)
            p = jnp.exp(s - m_new)
            l_ref[h] = alpha * l_ref[h] + _col_sum(p)
            m_ref[h] = m_new
            ps.append(p.astype(BF16))
            alphas.append(alpha)
        for h in range(B_HEADS):
            vt = bvt_ref[c, h * B_DH:(h + 1) * B_DH, :]
            acc_ref[h] = alphas[h] * acc_ref[h] + _dot(vt, ps[h])
        return carry

    lax.fori_loop(0, n_chunks, att_chunk, 0)
    out_t = jnp.concatenate([acc_ref[h] / l_ref[h] for h in range(B_HEADS)], axis=0)
    o_ref[...] = out_t.T.astype(o_ref.dtype)


def _dsa_prompt(iqt, iwt, ik2, bqt, bk, bvt):
    nb, s, _ = bk.shape
    lc = bvt.shape[-1]
    tq = _pick_tile(s, 256, LANES)
    topk = min(TOPK_MAX, s // 4)
    nbits = max(1, (s - 1).bit_length())
    col = lambda r: pl.BlockSpec((None, r, tq), lambda b, i: (b, 0, i))
    seq = lambda w: pl.BlockSpec((None, s, w), lambda b, i: (b, 0, 0))
    return pl.pallas_call(
        functools.partial(_dsa_kernel, tq=tq, lc=lc, topk=topk, nbits=nbits),
        out_shape=jax.ShapeDtypeStruct((nb, s, B_HEADS * B_DH), BF16),
        grid=(nb, s // tq),
        in_specs=[col(SLAB), col(IDX_HEADS), seq(MISC), col(SLAB), seq(SLAB),
                  pl.BlockSpec((None, s // lc, SLAB, lc), lambda b, i: (b, 0, 0, 0))],
        out_specs=pl.BlockSpec((None, tq, B_HEADS * B_DH), lambda b, i: (b, i, 0)),
        scratch_shapes=[pltpu.VMEM((s // lc, lc, tq), jnp.int32),
                        pltpu.VMEM((s // lc, lc, tq), F32),
                        pltpu.VMEM((1, tq), jnp.int32),
                        pltpu.VMEM((B_HEADS, 1, tq), F32), pltpu.VMEM((B_HEADS, 1, tq), F32),
                        pltpu.VMEM((B_HEADS, B_DH, tq), F32)],
        compiler_params=_cparams("parallel", "arbitrary"),
        name="dsa_prompt",
    )(iqt, iwt, ik2, bqt, bk, bvt)


def _page_specs(n, shape, layer, pps):
    zeros = (0,) * len(shape)

    def mk(u):
        return pl.BlockSpec((None, None) + shape,
                            lambda b, g, pt, u=u: (layer, pt[b, g * pps + u]) + zeros)
    return [mk(u) for u in range(n)]


def _diff_sample_kernel(pt_ref, q_ref, kn_ref, vn_ref, lam_ref, gain_ref, *rest, pps, lam_init):
    kt_refs = rest[:pps]
    v_refs = rest[pps:2 * pps]
    o_ref, m_ref, l_ref, acc_ref = rest[2 * pps:]
    g = pl.program_id(1)
    nmap = 2 * A_HEADS
    width = A_HEADS * A_DV

    @pl.when(g == 0)
    def _():
        m_ref[...] = jnp.full(m_ref.shape, NEG, F32)
        l_ref[...] = jnp.zeros(l_ref.shape, F32)
        acc_ref[...] = jnp.zeros(acc_ref.shape, F32)

    row = lax.broadcasted_iota(jnp.int32, (nmap, width), 0)
    lane = lax.broadcasted_iota(jnp.int32, (nmap, width), 1)
    qf = jnp.where(lane // A_DK == row, jnp.broadcast_to(q_ref[...].astype(F32), (nmap, width)), 0.0)
    qbd = qf.astype(BF16)
    head_of_row = lax.broadcasted_iota(jnp.int32, (nmap, A_DV), 0) // 2

    s = jnp.concatenate([_dot(qbd, kt_refs[u][...].astype(BF16)) for u in range(pps)], axis=1)
    m_prev = m_ref[...]
    m_new = jnp.maximum(m_prev, jnp.max(s, axis=-1, keepdims=True))
    alpha = jnp.exp(m_prev - m_new)
    p = jnp.exp(s - m_new)
    l_ref[...] = alpha * l_ref[...] + jnp.sum(p, axis=-1, keepdims=True)
    pv = jnp.zeros((nmap, A_DV), F32)
    for u in range(pps):
        pu = p[:, u * LANES:(u + 1) * LANES].astype(BF16)
        for hd in range(A_HEADS):
            vh = v_refs[u][pl.ds(hd, kt_refs[u].shape[-1], stride=A_HEADS), :].astype(BF16)
            pv = pv + jnp.where(head_of_row == hd, _dot(pu, vh), 0.0)
    acc_ref[...] = alpha * acc_ref[...] + pv
    m_ref[...] = m_new

    @pl.when(g == pl.num_programs(1) - 1)
    def _():
        s_self = jnp.sum(qf * kn_ref[...], axis=-1, keepdims=True)
        m_prev = m_ref[...]
        m_fin = jnp.maximum(m_prev, s_self)
        alpha = jnp.exp(m_prev - m_fin)
        p_self = jnp.exp(s_self - m_fin)
        l_fin = alpha * l_ref[...] + p_self
        vn = vn_ref[...]
        vn8 = jnp.zeros((nmap, A_DV), F32)
        for hd in range(A_HEADS):
            vn8 = vn8 + jnp.where(head_of_row == hd, vn[:, hd * A_DV:(hd + 1) * A_DV], 0.0)
        o = (alpha * acc_ref[...] + p_self * vn8) / l_fin
        lam = _lambda(lam_ref, lam_init)
        gain = gain_ref[...]
        for hd in range(A_HEADS):
            dh = o[2 * hd:2 * hd + 1, :] - lam * o[2 * hd + 1:2 * hd + 2, :]
            y = dh * lax.rsqrt(jnp.mean(dh * dh, axis=-1, keepdims=True) + EPS) * gain
            o_ref[:, hd * A_DV:(hd + 1) * A_DV] = (y * (1.0 - lam_init)).astype(o_ref.dtype)


def _diff_sample(page_table, aq, ak_new, av_new, lamvec, gain, cache_kt, cache_v, layer, lam_init):
    ns, npg = page_table.shape
    pps = _pick_tile(npg, 8, 1)
    page = cache_kt.shape[-1]
    width = A_HEADS * A_DV
    per_b = lambda w: pl.BlockSpec((None, 1, w), lambda b, g, pt: (b, 0, 0))
    full = lambda a: pl.BlockSpec(a.shape, lambda b, g, pt: (0,) * a.ndim)
    grid_spec = pltpu.PrefetchScalarGridSpec(
        num_scalar_prefetch=1,
        grid=(ns, npg // pps),
        in_specs=[per_b(width), per_b(width), per_b(width), full(lamvec), full(gain)]
        + _page_specs(pps, (width, page), layer, pps)
        + _page_specs(pps, (page * A_HEADS, A_DV), layer, pps),
        out_specs=per_b(width),
        scratch_shapes=[pltpu.VMEM((2 * A_HEADS, 1), F32), pltpu.VMEM((2 * A_HEADS, 1), F32),
                        pltpu.VMEM((2 * A_HEADS, A_DV), F32)],
    )
    return pl.pallas_call(
        functools.partial(_diff_sample_kernel, pps=pps, lam_init=lam_init),
        out_shape=jax.ShapeDtypeStruct((ns, 1, width), BF16),
        grid_spec=grid_spec,
        compiler_params=_cparams("parallel", "arbitrary"),
        name="diff_attn_sample",
    )(page_table, aq, ak_new, av_new, lamvec, gain, *([cache_kt] * pps), *([cache_v] * pps))


def _idx_sample_kernel(pt_ref, iq_ref, iw_ref, *rest, pps):
    ikt_refs = rest[:pps]
    o_ref = rest[pps]
    iq = iq_ref[...]
    w = iw_ref[...]
    rows = []
    for u in range(pps):
        lg = _dot(iq, ikt_refs[u][...].astype(BF16))
        rows.append(jnp.sum(jnp.maximum(lg, 0.0) * w, axis=0, keepdims=True) * IDX_SCALE)
    o_ref[...] = jnp.concatenate(rows, axis=0)


def _idx_sample(page_table, iq8, iw8, cache_ikt, layer):
    ns, npg = page_table.shape
    pps = _pick_tile(npg, 8, 8)
    page = cache_ikt.shape[-1]
    grid_spec = pltpu.PrefetchScalarGridSpec(
        num_scalar_prefetch=1,
        grid=(ns, npg // pps),
        in_specs=[pl.BlockSpec((None, IDX_HEADS, IDX_DIM), lambda b, g, pt: (b, 0, 0)),
                  pl.BlockSpec((None, IDX_HEADS, 1), lambda b, g, pt: (b, 0, 0))]
        + _page_specs(pps, (IDX_DIM, page), layer, pps),
        out_specs=pl.BlockSpec((None, pps, page), lambda b, g, pt: (b, g, 0)),
    )
    return pl.pallas_call(
        functools.partial(_idx_sample_kernel, pps=pps),
        out_shape=jax.ShapeDtypeStruct((ns, npg, page), F32),
        grid_spec=grid_spec,
        compiler_params=_cparams("parallel", "arbitrary"),
        name="idx_scores_sample",
    )(page_table, iq8, iw8, *([cache_ikt] * pps))


def _dsa_sample_kernel(pt_ref, sc_ref, iq_ref, iw_ref, ikn_ref, q_ref, kn_ref, vn_ref, *rest,
                       pps, topk, nbits, past):
    kt_refs = rest[:pps]
    vt_refs = rest[pps:2 * pps]
    o_ref, bias_ref, bself_ref, m_ref, l_ref, acc_ref = rest[2 * pps:]
    g = pl.program_id(1)
    width = B_HEADS * B_DH
    npg, page = sc_ref.shape

    @pl.when(g == 0)
    def _():
        m_ref[...] = jnp.full(m_ref.shape, NEG, F32)
        l_ref[...] = jnp.zeros(l_ref.shape, F32)
        acc_ref[...] = jnp.zeros(acc_ref.shape, F32)
        lg = jnp.sum(iq_ref[...].astype(F32) * ikn_ref[...].astype(BF16).astype(F32),
                     axis=-1, keepdims=True)
        s_self = jnp.sum(jnp.maximum(lg, 0.0) * iw_ref[...], axis=0, keepdims=True) * IDX_SCALE
        key_self = _sort_key(s_self)
        key = _sort_key(sc_ref[...])
        kpos = (lax.broadcasted_iota(jnp.int32, key.shape, 0) * page
                + lax.broadcasted_iota(jnp.int32, key.shape, 1))

        def total(mask, self_mask):
            c = jnp.sum(jnp.where(mask, 1, 0), axis=0, keepdims=True)
            return jnp.sum(c, axis=1, keepdims=True) + jnp.where(self_mask, 1, 0)

        thr = _kth_largest(lambda cand: total(key >= cand, key_self >= cand), topk, (1, 1))
        need = topk - total(key > thr, key_self > thr)
        jstar = _tie_cut(
            lambda cand: total(jnp.logical_and(key == thr, kpos < cand),
                               jnp.logical_and(key_self == thr, past < cand)),
            need, nbits, (1, 1))
        bias_ref[...] = jnp.where(
            key > thr, 0.0, jnp.where(key == thr, jnp.where(kpos <= jstar, 0.0, NEG), NEG))
        bself_ref[...] = jnp.where(
            key_self > thr, 0.0, jnp.where(key_self == thr, jnp.where(past <= jstar, 0.0, NEG), NEG))

    row = lax.broadcasted_iota(jnp.int32, (B_HEADS, width), 0)
    lane = lax.broadcasted_iota(jnp.int32, (B_HEADS, width), 1)
    qf = jnp.where(lane // B_DH == row, jnp.broadcast_to(q_ref[...].astype(F32), (B_HEADS, width)), 0.0)
    qbd = qf.astype(BF16)

    s = jnp.concatenate(
        [_dot(qbd, kt_refs[u][...].astype(BF16)) + bias_ref[pl.ds(g * pps + u, 1), :]
         for u in range(pps)], axis=1)
    m_prev = m_ref[...]
    m_new = jnp.maximum(m_prev, jnp.max(s, axis=-1, keepdims=True))
    alpha = jnp.exp(m_prev - m_new)
    p = jnp.exp(s - m_new)
    l_ref[...] = alpha * l_ref[...] + jnp.sum(p, axis=-1, keepdims=True)
    pv = jnp.zeros((B_HEADS, width), F32)
    for u in range(pps):
        pv = pv + _dot_nt(p[:, u * LANES:(u + 1) * LANES].astype(BF16), vt_refs[u][...].astype(BF16))
    acc_ref[...] = alpha * acc_ref[...] + pv
    m_ref[...] = m_new

    @pl.when(g == pl.num_programs(1) - 1)
    def _():
        s_self = jnp.sum(qf * kn_ref[...], axis=-1, keepdims=True) + bself_ref[...]
        m_prev = m_ref[...]
        m_fin = jnp.maximum(m_prev, s_self)
        alpha = jnp.exp(m_prev - m_fin)
        p_self = jnp.exp(s_self - m_fin)
        l_fin = alpha * l_ref[...] + p_self
        o = (alpha * acc_ref[...] + p_self * vn_ref[...]) / l_fin
        o_ref[...] = jnp.sum(jnp.where(lane // B_DH == row, o, 0.0), axis=0,
                             keepdims=True).astype(o_ref.dtype)


def _dsa_sample(page_table, scores, iq8, iw8, ik_new, bq, bk_new, bv_new, cache_kt, cache_vt, layer):
    ns, npg = page_table.shape
    pps = _pick_tile(npg, 8, 1)
    page = cache_kt.shape[-1]
    past = npg * page
    width = B_HEADS * B_DH
    topk = min(TOPK_MAX, (past + 1) // 4)
    nbits = max(1, past.bit_length())
    per_b = lambda r, w: pl.BlockSpec((None, r, w), lambda b, g, pt: (b, 0, 0))
    grid_spec = pltpu.PrefetchScalarGridSpec(
        num_scalar_prefetch=1,
        grid=(ns, npg // pps),
        in_specs=[per_b(npg, page), per_b(IDX_HEADS, IDX_DIM), per_b(IDX_HEADS, 1), per_b(1, IDX_DIM),
                  per_b(1, width), per_b(1, width), per_b(1, width)]
        + _page_specs(pps, (width, page), layer, pps) + _page_specs(pps, (width, page), layer, pps),
        out_specs=per_b(1, width),
        scratch_shapes=[pltpu.VMEM((npg, page), F32), pltpu.VMEM((1, 1), F32),
                        pltpu.VMEM((B_HEADS, 1), F32), pltpu.VMEM((B_HEADS, 1), F32),
                        pltpu.VMEM((B_HEADS, width), F32)],
    )
    return pl.pallas_call(
        functools.partial(_dsa_sample_kernel, pps=pps, topk=topk, nbits=nbits, past=past),
        out_shape=jax.ShapeDtypeStruct((ns, 1, width), BF16),
        grid_spec=grid_spec,
        compiler_params=_cparams("parallel", "arbitrary"),
        name="dsa_sample",
    )(page_table, scores, iq8, iw8, ik_new, bq, bk_new, bv_new,
      *([cache_kt] * pps), *([cache_vt] * pps))


def _outproj_kernel(*refs, route):
    if route:
        (ao_ref, bo_ref, w_ref, x_ref, g1_ref, sc_ref, sh_ref, gf_ref, wr_ref, br_ref,
         x1_ref, h2_ref, comb_ref) = refs
    else:
        ao_ref, bo_ref, w_ref, x_ref, g1_ref, sc_ref, sh_ref, gf_ref, x1_ref, h2_ref = refs
    wa = ao_ref.shape[-1]
    mix = _dot(ao_ref[...], w_ref[:wa, :]) + _dot(bo_ref[...], w_ref[wa:, :])
    x1 = x_ref[...] + g1_ref[...] * mix
    x1_ref[...] = x1
    h = x1 * lax.rsqrt(jnp.mean(x1 * x1, axis=-1, keepdims=True) + EPS) * gf_ref[...]
    h = h * (1.0 + sc_ref[...]) + sh_ref[...]
    h2_ref[...] = h.astype(BF16)
    if route:
        logits = jnp.dot(h, wr_ref[...], precision=HIGHEST, preferred_element_type=F32) + br_ref[...]
        lane = lax.broadcasted_iota(jnp.int32, logits.shape, 1)
        logits = jnp.where(lane < N_EXPERTS, logits, -jnp.inf)
        v1 = jnp.max(logits, axis=-1, keepdims=True)
        i1 = jnp.min(jnp.where(logits == v1, lane, LANES), axis=-1, keepdims=True)
        rest = jnp.where(lane == i1, -jnp.inf, logits)
        v2 = jnp.max(rest, axis=-1, keepdims=True)
        i2 = jnp.min(jnp.where(rest == v2, lane, LANES), axis=-1, keepdims=True)
        e2 = jnp.exp(v2 - v1)
        den = 1.0 + e2
        comb_ref[...] = jnp.where(lane == i1, 1.0 / den, jnp.where(lane == i2, e2 / den, 0.0))


def _outproj(ao, bo, w_bf, x3, gate1, scale2, shift2, gf, router=None):
    nb, s, d = x3.shape
    tm = _pick_tile(s, 512, 16)
    row = lambda w: pl.BlockSpec((None, tm, w), lambda b, i: (b, i, 0))
    full = lambda a: pl.BlockSpec(a.shape, lambda b, i: (0,) * a.ndim)
    ins = [ao, bo, w_bf, x3, gate1, scale2, shift2, gf]
    in_specs = [row(ao.shape[-1]), row(bo.shape[-1]), full(w_bf), row(d), _mod_spec(gate1, tm),
                _mod_spec(scale2, tm), _mod_spec(shift2, tm), full(gf)]
    out_shape = [jax.ShapeDtypeStruct((nb, s, d), F32), jax.ShapeDtypeStruct((nb, s, d), BF16)]
    out_specs = [row(d), row(d)]
    if router is not None:
        ins += list(router)
        in_specs += [full(router[0]), full(router[1])]
        out_shape.append(jax.ShapeDtypeStruct((nb, s, LANES), F32))
        out_specs.append(row(LANES))
    return pl.pallas_call(
        functools.partial(_outproj_kernel, route=router is not None),
        out_shape=tuple(out_shape),
        grid=(nb, s // tm),
        in_specs=in_specs,
        out_specs=tuple(out_specs),
        compiler_params=_cparams("parallel", "parallel"),
        name="out_proj",
    )(*ins)


def _ffn_kernel(h_ref, w1_ref, w3_ref, w2_ref, x_ref, g2_ref, o_ref, acc_ref):
    j = pl.program_id(2)

    @pl.when(j == 0)
    def _():
        acc_ref[...] = jnp.zeros(acc_ref.shape, F32)

    h = h_ref[...]
    t = jax.nn.silu(_dot(h, w1_ref[...])) * _dot(h, w3_ref[...])
    acc_ref[...] += _dot(t.astype(BF16), w2_ref[...])

    @pl.when(j == pl.num_programs(2) - 1)
    def _():
        o_ref[...] = x_ref[...] + g2_ref[...] * acc_ref[...]


def _ffn_dense(h2, w1, w3, w2, x1, gate2):
    nb, s, d = x1.shape
    ff = w1.shape[1]
    tm = _pick_tile(s, 512, 16)
    tf = _pick_tile(ff, 1408, LANES)
    row = pl.BlockSpec((None, tm, d), lambda b, i, j: (b, i, 0))
    return pl.pallas_call(
        _ffn_kernel,
        out_shape=jax.ShapeDtypeStruct((nb, s, d), F32),
        grid=(nb, s // tm, ff // tf),
        in_specs=[row, pl.BlockSpec((d, tf), lambda b, i, j: (0, j)),
                  pl.BlockSpec((d, tf), lambda b, i, j: (0, j)),
                  pl.BlockSpec((tf, d), lambda b, i, j: (j, 0)), row, _mod_spec(gate2, tm)],
        out_specs=row,
        scratch_shapes=[pltpu.VMEM((tm, d), F32)],
        compiler_params=_cparams("parallel", "parallel", "arbitrary"),
        name="ffn_dense",
    )(h2, w1, w3, w2, x1, gate2)


def _moe_kernel(h_ref, comb_ref, w1_ref, w3_ref, w2_ref, x_ref, g2_ref, o_ref, acc_ref):
    e = pl.program_id(2)
    j = pl.program_id(3)

    @pl.when(jnp.logical_and(e == 0, j == 0))
    def _():
        acc_ref[...] = jnp.zeros(acc_ref.shape, F32)

    comb = comb_ref[...]
    lane = lax.broadcasted_iota(jnp.int32, comb.shape, 1)
    ce = jnp.sum(jnp.where(lane == e, comb, 0.0), axis=-1, keepdims=True)
    h = h_ref[...]
    t = jax.nn.silu(_dot(h, w1_ref[...])) * _dot(h, w3_ref[...])
    acc_ref[...] += ce * _dot(t.astype(BF16), w2_ref[...])

    @pl.when(jnp.logical_and(e == pl.num_programs(2) - 1, j == pl.num_programs(3) - 1))
    def _():
        o_ref[...] = x_ref[...] + g2_ref[...] * acc_ref[...]


def _ffn_moe(h2, comb, w1, w3, w2, x1, gate2):
    nb, s, d = x1.shape
    ne, _, ff = w1.shape
    tm = _pick_tile(s, 512, 16)
    tf = _pick_tile(ff, 1792, LANES)
    row = lambda w: pl.BlockSpec((None, tm, w), lambda b, i, e, j: (b, i, 0))
    return pl.pallas_call(
        _moe_kernel,
        out_shape=jax.ShapeDtypeStruct((nb, s, d), F32),
        grid=(nb, s // tm, ne, ff // tf),
        in_specs=[row(d), row(LANES),
                  pl.BlockSpec((None, d, tf), lambda b, i, e, j: (e, 0, j)),
                  pl.BlockSpec((None, d, tf), lambda b, i, e, j: (e, 0, j)),
                  pl.BlockSpec((None, tf, d), lambda b, i, e, j: (e, j, 0)), row(d),
                  _mod_spec(gate2, tm)],
        out_specs=row(d),
        scratch_shapes=[pltpu.VMEM((tm, d), F32)],
        compiler_params=_cparams("parallel", "parallel", "arbitrary", "arbitrary"),
        name="ffn_moe",
    )(h2, comb, w1, w3, w2, x1, gate2)


def kernel(x_prompt, x_sample, cache_a_k, cache_a_v, cache_b_k, cache_b_v, cache_b_ik, page_table, c_prompt, c_sample, w_ada, b_ada, g_mix, g_ffn, w_in, w_out, a_q_norm, a_k_norm, b_q_norm, b_k_norm, lam_q1, lam_k1, lam_q2, lam_k2, a_sub_norm, w1_dense, w3_dense, w2_dense, w_router, b_router, w1_exp, w3_exp, w2_exp):
    nb, s, d = x_prompt.shape
    ns, dec_seq, _ = x_sample.shape
    assert dec_seq == 1
    depth = w_in.shape[0]
    n_pool, page = cache_a_k.shape[1:3]
    past = page_table.shape[1] * page
    assert w_in.shape[2] == N_SLABS * SLAB + IDX_DIM + IDX_HEADS

    rows = nb + ns
    rpad = -rows % 8
    c_all = jnp.concatenate([c_prompt, c_sample, jnp.zeros((rpad, d), F32)], axis=0)
    mod = _ada(c_all, w_ada, b_ada)

    tabs_p = _rope_tables(jnp.arange(s, dtype=jnp.int32))
    tabs_s = _rope_tables(jnp.full((ns,), past, jnp.int32))
    gidx = jnp.arange(SLAB) // HEAD_DIM
    m64 = (gidx[:, None] == gidx[None, :]).astype(BF16) * (1.0 / HEAD_DIM)

    ca_kt = jnp.transpose(cache_a_k, (0, 1, 3, 4, 5, 2)).reshape(depth, n_pool, A_HEADS * 2 * A_DK, page)
    ca_v = cache_a_v.reshape(depth, n_pool, page * A_HEADS, A_DV)
    cb_kt = jnp.transpose(cache_b_k, (0, 1, 3, 4, 2)).reshape(depth, n_pool, B_HEADS * B_DH, page)
    cb_vt = jnp.transpose(cache_b_v, (0, 1, 3, 4, 2)).reshape(depth, n_pool, B_HEADS * B_DH, page)
    cb_ikt = jnp.transpose(cache_b_ik, (0, 1, 3, 2))

    xp = x_prompt
    xs = x_sample.reshape(1, ns, d)
    outs_p, outs_s = [], []
    for l in range(depth):
        lam_init = 0.8 - 0.6 * math.exp(-0.3 * l)
        w_in_bf = jnp.pad(w_in[l], ((0, 0), (0, NPROJ - w_in.shape[2]))).astype(BF16)
        w_out_bf = w_out[l].astype(BF16)
        rep = SLAB // HEAD_DIM
        gains = jnp.stack([jnp.tile(a_q_norm[l], rep), jnp.tile(a_k_norm[l], rep),
                           jnp.tile(b_q_norm[l], rep), jnp.tile(b_k_norm[l], rep)])
        lamvec = jnp.stack([lam_q1[l], lam_k1[l], lam_q2[l], lam_k2[l]])
        sub_gain = a_sub_norm[l].reshape(1, A_DV)
        sub_gain_col = a_sub_norm[l].reshape(A_DV, 1)
        g_mix_l = g_mix[l].reshape(1, d)
        g_ffn_l = g_ffn[l].reshape(1, d)
        mods_p = [m.reshape(nb, 1, d) for m in jnp.split(mod[l, :nb], 6, axis=-1)]
        mods_s = [m.reshape(1, ns, d) for m in jnp.split(mod[l, nb:rows], 6, axis=-1)]
        moe = l % 2 == 1
        jj = l // 2
        if moe:
            router = (jnp.pad(w_router[jj], ((0, 0), (0, LANES - N_EXPERTS))),
                      jnp.pad(b_router[jj], (0, LANES - N_EXPERTS)).reshape(1, LANES))
            ffw = (w1_exp[jj].astype(BF16), w3_exp[jj].astype(BF16), w2_exp[jj].astype(BF16))
        else:
            router = None
            ffw = (w1_dense[jj].astype(BF16), w3_dense[jj].astype(BF16), w2_dense[jj].astype(BF16))

        def mix_and_ffn(x3, mods, ao, bo):
            shift1, scale1, gate1, shift2, scale2, gate2 = mods
            res = _outproj(ao, bo, w_out_bf, x3, gate1, scale2, shift2, g_ffn_l, router)
            if moe:
                x1, h2, comb = res
                return _ffn_moe(h2, comb, *ffw, x1, gate2)
            x1, h2 = res
            return _ffn_dense(h2, *ffw, x1, gate2)

        shift1, scale1 = mods_p[0], mods_p[1]
        (akf, avf, bkf, bvf, miscf, akb, bkb, ik2b, aqt, avt, bqt, bvt, iqt, iwt) = _project(
            xp, scale1, shift1, g_mix_l, w_in_bf, gains, tabs_p, m64, True)
        ao = _diff_prompt(aqt, akb, avt, lamvec, sub_gain_col, lam_init)
        bo = _dsa_prompt(iqt, iwt, ik2b, bqt, bkb, bvt)
        xp = mix_and_ffn(xp, mods_p, ao, bo)
        outs_p.append((akf.reshape(nb, s, A_HEADS, 2, A_DK), avf.reshape(nb, s, A_HEADS, A_DV),
                       bkf.reshape(nb, s, B_HEADS, B_DH), bvf.reshape(nb, s, B_HEADS, B_DH),
                       miscf[..., :IDX_DIM]))

        shift1, scale1 = mods_s[0], mods_s[1]
        (akf, avf, bkf, bvf, miscf, aqb, bqb, iqb) = _project(
            xs, scale1, shift1, g_mix_l, w_in_bf, gains, tabs_s, m64, False)
        per_tok = lambda a: a.reshape(ns, 1, a.shape[-1])
        ao = _diff_sample(page_table, per_tok(aqb), per_tok(akf), per_tok(avf), lamvec, sub_gain,
                          ca_kt, ca_v, l, lam_init)
        iq8 = iqb.reshape(ns, IDX_HEADS, IDX_DIM)
        iw8 = miscf[0, :, IW_LANE0:IW_LANE0 + IDX_HEADS].reshape(ns, IDX_HEADS, 1)
        ik_new = miscf[0, :, :IDX_DIM].reshape(ns, 1, IDX_DIM)
        scores = _idx_sample(page_table, iq8, iw8, cb_ikt, l)
        bo = _dsa_sample(page_table, scores, iq8, iw8, ik_new, per_tok(bqb), per_tok(bkf),
                         per_tok(bvf), cb_kt, cb_vt, l)
        xs = mix_and_ffn(xs, mods_s, ao.reshape(1, ns, -1), bo.reshape(1, ns, -1))
        outs_s.append((akf.reshape(ns, 1, A_HEADS, 2, A_DK), avf.reshape(ns, 1, A_HEADS, A_DV),
                       bkf.reshape(ns, 1, B_HEADS, B_DH), bvf.reshape(ns, 1, B_HEADS, B_DH),
                       miscf[0, :, :IDX_DIM].reshape(ns, 1, IDX_DIM)))

    stack = lambda rows_, i: jnp.stack([r[i] for r in rows_])
    return (xp, xs.reshape(ns, 1, d),
            stack(outs_p, 0), stack(outs_p, 1), stack(outs_p, 2), stack(outs_p, 3), stack(outs_p, 4),
            stack(outs_s, 0), stack(outs_s, 1), stack(outs_s, 2), stack(outs_s, 3), stack(outs_s, 4))
```

```python
import functools
import math

import jax
import jax.numpy as jnp
from jax import lax
from jax.experimental import pallas as pl
from jax.experimental.pallas import tpu as pltpu

F32 = jnp.float32
BF16 = jnp.bfloat16
HIGHEST = lax.Precision.HIGHEST

A_HEADS = 4
A_DK = 64
A_DV = 2 * A_DK
B_HEADS = 8
B_DH = 64
IDX_HEADS = 8
IDX_DIM = 64
IDX_SCALE = (IDX_HEADS ** -0.5) * (IDX_DIM ** -0.5)
TOPK_MAX = 256
ROPE_THETA = 500000.0
ROPE_FRACTION = 4
N_EXPERTS = 8
EPS = 1e-6

HEAD_DIM = 64
ROPE_HALF = HEAD_DIM // ROPE_FRACTION // 2
SLAB = 512
N_SLABS = 7
MISC = 128
IW_LANE0 = IDX_DIM
NPROJ = N_SLABS * SLAB + MISC
LANES = 128
INT_MIN = -2 ** 31
INT_MAX = 2 ** 31 - 1
NEG = -1e30
VMEM_LIMIT = 56 * 1024 * 1024


def _dot(a, b):
    return jnp.dot(a, b, preferred_element_type=F32)


def _dot_nt(a, b):
    return lax.dot_general(a, b, (((1,), (1,)), ((), ())), preferred_element_type=F32)


def _cparams(*sem):
    return pltpu.CompilerParams(dimension_semantics=sem, vmem_limit_bytes=VMEM_LIMIT)


def _col_reduce(x, reduce_fn, combine_fn, parts=8):
    n = x.shape[0]
    parts = parts if n % (8 * parts) == 0 else 1
    step = n // parts
    vals = [reduce_fn(x[k * step:(k + 1) * step], axis=0, keepdims=True) for k in range(parts)]
    while len(vals) > 1:
        vals = [combine_fn(vals[k], vals[k + 1]) for k in range(0, len(vals), 2)]
    return vals[0]


def _col_max(x):
    return _col_reduce(x, jnp.max, jnp.maximum)


def _col_sum(x):
    return _col_reduce(x, jnp.sum, jnp.add)


def _pick_tile(n, target, mult):
    best = None
    for t in range(mult, min(n, target) + 1, mult):
        if n % t == 0:
            best = t
    return best if best is not None else n


def _ada_kernel(c_ref, w_ref, b_ref, o_ref):
    c = c_ref[...]
    o_ref[...] = jnp.dot(jax.nn.silu(c), w_ref[...], precision=HIGHEST,
                         preferred_element_type=F32) + b_ref[...]


def _ada(c_all, w_ada, b_ada):
    depth, d, n = w_ada.shape
    r = c_all.shape[0]
    tn = _pick_tile(n, 1536, LANES)
    return pl.pallas_call(
        _ada_kernel,
        out_shape=jax.ShapeDtypeStruct((depth, r, n), F32),
        grid=(depth, n // tn),
        in_specs=[pl.BlockSpec((r, d), lambda l, j: (0, 0)),
                  pl.BlockSpec((None, d, tn), lambda l, j: (l, 0, j)),
                  pl.BlockSpec((None, 1, tn), lambda l, j: (l, 0, j))],
        out_specs=pl.BlockSpec((None, r, tn), lambda l, j: (l, 0, j)),
        compiler_params=_cparams("parallel", "parallel"),
        name="adaln",
    )(c_all, w_ada, b_ada.reshape(depth, 1, n))


def _rope_tables(pos):
    r = HEAD_DIM // ROPE_FRACTION
    inv = ROPE_THETA ** (-jnp.arange(ROPE_HALF, dtype=F32) * 2.0 / r)
    ang = pos.astype(F32)[:, None] * inv[None, :]
    cos, sin = jnp.cos(ang), jnp.sin(ang)
    s = pos.shape[0]
    rest = HEAD_DIM - 2 * ROPE_HALF
    z8 = jnp.zeros((s, ROPE_HALF), F32)
    zr = jnp.zeros((s, rest), F32)
    c = jnp.concatenate([cos, cos, jnp.ones((s, rest), F32)], axis=1)
    s1 = jnp.concatenate([-sin, z8, zr], axis=1)
    s2 = jnp.concatenate([z8, sin, zr], axis=1)
    rep = LANES // HEAD_DIM
    return tuple(jnp.concatenate([t] * rep, axis=1) for t in (c, s1, s2))


def _proj_kernel(x_ref, sc_ref, sh_ref, g_ref, w_ref, gains_ref, rc_ref, rs1_ref, rs2_ref,
                 m64_ref, *rest, prompt):
    if prompt:
        akf, avf, bkf, bvf, miscf, akb, bkb, ik2b, aqt, avt, bqt, bvt, iqt, iwt, h_scr = rest
    else:
        akf, avf, bkf, bvf, miscf, aqb, bqb, iqb, h_scr = rest
    x = x_ref[...]
    h = x * lax.rsqrt(jnp.mean(x * x, axis=-1, keepdims=True) + EPS) * g_ref[...]
    h = h * (1.0 + sc_ref[...]) + sh_ref[...]
    h_scr[...] = h.astype(BF16)

    c1, s1, s2 = rc_ref[...], rs1_ref[...], rs2_ref[...]
    rep = SLAB // LANES
    c4 = jnp.concatenate([c1] * rep, axis=1)
    s14 = jnp.concatenate([s1] * rep, axis=1)
    s24 = jnp.concatenate([s2] * rep, axis=1)
    m64 = m64_ref[...]

    def slab(i, width=SLAB):
        return _dot(h_scr[...], w_ref[:, i * SLAB:i * SLAB + width])

    def qknorm(p, gi):
        sq = p * p
        hi = sq.astype(BF16)
        lo = (sq - hi.astype(F32)).astype(BF16)
        ms = _dot(hi, m64) + _dot(lo, m64)
        return p * lax.rsqrt(ms + EPS) * gains_ref[gi:gi + 1, :]

    def rope(y, c, a, b):
        n = y.shape[-1]
        return y * c + pltpu.roll(y, n - ROPE_HALF, 1) * a + pltpu.roll(y, ROPE_HALF, 1) * b

    def put(val, row_ref=None, t_ref=None):
        if row_ref is not None:
            row_ref[...] = val.astype(row_ref.dtype)
        if t_ref is not None:
            t_ref[...] = val.T.astype(t_ref.dtype)

    aq = rope(qknorm(slab(0), 0), c4, s14, s24) * (A_DK ** -0.5)
    put(aq, None, aqt) if prompt else put(aq, aqb)
    ak = rope(qknorm(slab(1), 1), c4, s14, s24)
    akf[...] = ak
    if prompt:
        put(ak, akb)
    av = slab(2)
    avf[...] = av
    if prompt:
        put(av, None, avt)
    bq = rope(qknorm(slab(3), 2), c4, s14, s24) * (B_DH ** -0.5)
    put(bq, None, bqt) if prompt else put(bq, bqb)
    bk = rope(qknorm(slab(4), 3), c4, s14, s24)
    bkf[...] = bk
    if prompt:
        put(bk, bkb)
    bv = slab(5)
    bvf[...] = bv
    if prompt:
        put(bv, None, bvt)
    iq = rope(slab(6), c4, s14, s24)
    put(iq, None, iqt) if prompt else put(iq, iqb)
    pm = slab(N_SLABS, MISC)
    lane = lax.broadcasted_iota(jnp.int32, pm.shape, 1)
    misc = jnp.where(lane < IDX_DIM, rope(pm, c1, s1, s2), pm)
    miscf[...] = misc
    if prompt:
        ik2b[...] = jnp.where(lane < IDX_DIM, misc, pltpu.roll(misc, IDX_DIM, 1)).astype(BF16)
        iwt[...] = misc.T[IW_LANE0:IW_LANE0 + IDX_HEADS, :]


def _mod_spec(arr, tm):
    _, r, d = arr.shape
    if r == 1:
        return pl.BlockSpec((None, 1, d), lambda b, i, *_: (b, 0, 0))
    return pl.BlockSpec((None, tm, d), lambda b, i, *_: (b, i, 0))


def _proj_tile(s):
    return _pick_tile(s, 512, LANES if s % LANES == 0 else 16)


def _project(x3, scale, shift, g, w_bf, gains, tabs, m64, prompt):
    nb, s, d = x3.shape
    tm = _proj_tile(s)
    rc, rs1, rs2 = tabs
    row = lambda w: pl.BlockSpec((None, tm, w), lambda b, i: (b, i, 0))
    col = lambda r: pl.BlockSpec((None, r, tm), lambda b, i: (b, 0, i))
    tab = pl.BlockSpec((tm, LANES), lambda b, i: (i, 0))
    full = lambda a: pl.BlockSpec(a.shape, lambda b, i: (0,) * a.ndim)
    rows = lambda w, dt: (jax.ShapeDtypeStruct((nb, s, w), dt), row(w))
    cols = lambda r, dt: (jax.ShapeDtypeStruct((nb, r, s), dt), col(r))
    tiled = (jax.ShapeDtypeStruct((nb, s // tm, SLAB, tm), BF16),
             pl.BlockSpec((None, None, SLAB, tm), lambda b, i: (b, i, 0, 0)))
    outs = [rows(SLAB, F32)] * 4 + [rows(MISC, F32)]
    if prompt:
        outs += ([rows(SLAB, BF16)] * 2 + [rows(MISC, BF16)] + [cols(SLAB, BF16)] * 3 + [tiled]
                 + [cols(SLAB, BF16), cols(IDX_HEADS, F32)])
    else:
        outs += [rows(SLAB, BF16)] * 3
    return pl.pallas_call(
        functools.partial(_proj_kernel, prompt=prompt),
        out_shape=tuple(o[0] for o in outs),
        grid=(nb, s // tm),
        in_specs=[row(d), _mod_spec(scale, tm), _mod_spec(shift, tm), full(g), full(w_bf),
                  full(gains), tab, tab, tab, full(m64)],
        out_specs=tuple(o[1] for o in outs),
        scratch_shapes=[pltpu.VMEM((tm, d), BF16)],
        compiler_params=_cparams("parallel", "parallel"),
        name="in_proj",
    )(x3, scale, shift, g, w_bf, gains, rc, rs1, rs2, m64)


def _lambda(lam_ref, lam_init):
    lv = lam_ref[...]
    a = jnp.sum(lv[0:1] * lv[1:2], axis=-1, keepdims=True)
    b = jnp.sum(lv[2:3] * lv[3:4], axis=-1, keepdims=True)
    return jnp.exp(a) - jnp.exp(b) + lam_init


def _diff_kernel(qi_ref, ki_ref, qt_ref, k_ref, vt_ref, lam_ref, gain_ref, o_ref,
                 m_ref, l_ref, acc_ref, *, lam_init):
    p_id = pl.program_id(2)
    qi = qi_ref[p_id]
    ki = ki_ref[p_id]

    @pl.when(ki == 0)
    def _():
        m_ref[...] = jnp.full(m_ref.shape, NEG, F32)
        l_ref[...] = jnp.zeros(l_ref.shape, F32)
        acc_ref[...] = jnp.zeros(acc_ref.shape, F32)

    qt = qt_ref[...]
    row = lax.broadcasted_iota(jnp.int32, qt.shape, 0)
    zero = jnp.zeros_like(qt)
    qs = (jnp.where(row < A_DK, qt, zero), jnp.where(row >= A_DK, qt, zero))
    k = k_ref[...]
    vt = vt_ref[...]

    def update(masked):
        ss = [_dot(k, qs[mp]) for mp in range(2)]
        ps, alphas = [], []
        for mp in range(2):
            s = ss[mp]
            if masked:
                kpos = lax.broadcasted_iota(jnp.int32, s.shape, 0)
                qpos = lax.broadcasted_iota(jnp.int32, s.shape, 1)
                s = jnp.where(kpos <= qpos, s, NEG)
            m_prev = m_ref[mp]
            m_new = jnp.maximum(m_prev, _col_max(s))
            alpha = jnp.exp(m_prev - m_new)
            p = jnp.exp(s - m_new)
            l_ref[mp] = alpha * l_ref[mp] + _col_sum(p)
            m_ref[mp] = m_new
            ps.append(p.astype(BF16))
            alphas.append(alpha)
        for mp in range(2):
            acc_ref[mp] = alphas[mp] * acc_ref[mp] + _dot(vt, ps[mp])

    @pl.when(ki < qi)
    def _():
        update(False)

    @pl.when(ki == qi)
    def _():
        update(True)
        lam = _lambda(lam_ref, lam_init)
        o = acc_ref[0] / l_ref[0] - lam * (acc_ref[1] / l_ref[1])
        y = o * lax.rsqrt(jnp.mean(o * o, axis=0, keepdims=True) + EPS) * gain_ref[...]
        o_ref[...] = (y * (1.0 - lam_init)).T.astype(o_ref.dtype)


def _diff_prompt(aqt, ak, avt, lamvec, gain_col, lam_init):
    nb, s, _ = ak.shape
    t = _pick_tile(s, 512, LANES)
    nq = s // t
    pairs = [(i, j) for i in range(nq) for j in range(i + 1)]
    qi = jnp.asarray([p[0] for p in pairs], jnp.int32)
    ki = jnp.asarray([p[1] for p in pairs], jnp.int32)
    full = lambda a: pl.BlockSpec(a.shape, lambda b, h, p, qi, ki: (0,) * a.ndim)
    grid_spec = pltpu.PrefetchScalarGridSpec(
        num_scalar_prefetch=2,
        grid=(nb, A_HEADS, len(pairs)),
        in_specs=[pl.BlockSpec((None, A_DV, t), lambda b, h, p, qi, ki: (b, h, qi[p])),
                  pl.BlockSpec((None, t, A_DV), lambda b, h, p, qi, ki: (b, ki[p], h)),
                  pl.BlockSpec((None, A_DV, t), lambda b, h, p, qi, ki: (b, h, ki[p])),
                  full(lamvec), full(gain_col)],
        out_specs=pl.BlockSpec((None, t, A_DV), lambda b, h, p, qi, ki: (b, qi[p], h)),
        scratch_shapes=[pltpu.VMEM((2, 1, t), F32), pltpu.VMEM((2, 1, t), F32),
                        pltpu.VMEM((2, A_DV, t), F32)],
    )
    return pl.pallas_call(
        functools.partial(_diff_kernel, lam_init=lam_init),
        out_shape=jax.ShapeDtypeStruct((nb, s, A_HEADS * A_DV), BF16),
        grid_spec=grid_spec,
        compiler_params=_cparams("parallel", "parallel", "arbitrary"),
        name="diff_attn_prompt",
    )(qi, ki, aqt, ak, avt, lamvec, gain_col)


def _sort_key(score):
    score = jnp.where(score == 0.0, 0.0, score)
    bits = lax.bitcast_convert_type(score, jnp.int32)
    return jnp.where(bits < 0, bits ^ jnp.int32(INT_MAX), bits)


def _kth_largest(count_ge, k, shape, settled=None):
    def pending(cnt):
        p = cnt != k
        return p if settled is None else jnp.logical_and(p, jnp.logical_not(settled))

    def cond(st):
        i, _, cnt = st
        return jnp.logical_and(i < 32, jnp.max(jnp.where(pending(cnt), 1, 0)) > 0)

    def body(st):
        i, tu, cnt = st
        cand_u = tu | lax.shift_left(jnp.int32(1), 31 - i)
        c = count_ge(cand_u ^ jnp.int32(INT_MIN))
        take = c >= k
        return i + 1, jnp.where(take, cand_u, tu), jnp.where(take, c, cnt)

    _, tu, _ = lax.while_loop(
        cond, body, (jnp.int32(0), jnp.zeros(shape, jnp.int32), jnp.full(shape, -1, jnp.int32)))
    return tu ^ jnp.int32(INT_MIN)


def _tie_cut(count_eq_below, need, nbits, shape):
    def body(i, j):
        cand = j | lax.shift_left(jnp.int32(1), nbits - 1 - i)
        return jnp.where(count_eq_below(cand) < need, cand, j)
    return lax.fori_loop(0, nbits, body, jnp.zeros(shape, jnp.int32))


def _dsa_kernel(iqt_ref, iwt_ref, ik2_ref, bqt_ref, bk_ref, bvt_ref, o_ref,
                key_ref, bias_ref, jstar_ref, m_ref, l_ref, acc_ref, *, tq, lc, topk, nbits):
    i = pl.program_id(1)
    q0 = i * tq
    n_chunks = (q0 + tq + lc - 1) // lc
    row128 = lax.broadcasted_iota(jnp.int32, (LANES, tq), 0)
    low = row128 < HEAD_DIM

    def head_block(ref, h):
        blk = ref[(h // 2) * LANES:(h // 2 + 1) * LANES, :]
        keep = low if h % 2 == 0 else jnp.logical_not(low)
        return jnp.where(keep, blk, jnp.zeros_like(blk))

    kpos0 = lax.broadcasted_iota(jnp.int32, (lc, tq), 0)
    qpos = q0 + lax.broadcasted_iota(jnp.int32, (lc, tq), 1)

    iw = iwt_ref[...]
    iqh = [head_block(iqt_ref, h) for h in range(IDX_HEADS)]

    def score_chunk(c, carry):
        off = pl.multiple_of(c * lc, lc)
        ikc = ik2_ref[pl.ds(off, lc), :]
        sc = jnp.zeros((lc, tq), F32)
        for h in range(IDX_HEADS):
            sc = sc + jnp.maximum(_dot(ikc, iqh[h]), 0.0) * iw[h:h + 1, :]
        key = _sort_key(sc * IDX_SCALE)
        key_ref[c] = jnp.where(off + kpos0 <= qpos, key, jnp.int32(INT_MIN))
        return carry

    lax.fori_loop(0, n_chunks, score_chunk, 0)

    def count(pred):
        def body(c, acc):
            return acc + _col_sum(jnp.where(pred(key_ref[c], c * lc), 1, 0))
        return lax.fori_loop(0, n_chunks, body, jnp.zeros((1, tq), jnp.int32))

    few = q0 + lax.broadcasted_iota(jnp.int32, (1, tq), 1) < topk
    thr = _kth_largest(lambda cand: count(lambda kc, _: kc >= cand), topk, (1, tq), few)
    cnt_ge = count(lambda kc, _: kc >= thr)
    straddle = jnp.logical_and(cnt_ge > topk, thr != jnp.int32(INT_MIN))
    jstar_ref[...] = jnp.full((1, tq), INT_MAX, jnp.int32)

    @pl.when(jnp.max(jnp.where(straddle, 1, 0)) > 0)
    def _():
        need = topk - count(lambda kc, _: kc > thr)

        def eq_below(cand):
            return count(lambda kc, base: jnp.logical_and(kc == thr, base + kpos0 < cand))

        jstar_ref[...] = _tie_cut(eq_below, need, nbits, (1, tq))

    jstar = jstar_ref[...]

    def bias_chunk(c, carry):
        kc = key_ref[c]
        kpos = c * lc + kpos0
        b = jnp.where(kc > thr, 0.0, jnp.where(kc == thr, jnp.where(kpos <= jstar, 0.0, NEG), NEG))
        bias_ref[c] = jnp.where(kc == jnp.int32(INT_MIN), NEG, b)
        return carry

    lax.fori_loop(0, n_chunks, bias_chunk, 0)

    m_ref[...] = jnp.full(m_ref.shape, NEG, F32)
    l_ref[...] = jnp.zeros(l_ref.shape, F32)
    acc_ref[...] = jnp.zeros(acc_ref.shape, F32)
    bqh = [head_block(bqt_ref, h) for h in range(B_HEADS)]

    def att_chunk(c, carry):
        off = pl.multiple_of(c * lc, lc)
        bias = bias_ref[c]
        kcs = [bk_ref[pl.ds(off, lc), j * LANES:(j + 1) * LANES] for j in range(B_HEADS // 2)]
        ss = [_dot(kcs[h // 2], bqh[h]) for h in range(B_HEADS)]
        ps, alphas = [], []
        for h in range(B_HEADS):
            s = ss[h] + bias
            m_prev = m_ref[h]
            m_new = jnp.maximum(m_prev, _col_max(s))
            alpha = jnp.exp(m_prev - m_new)
            p = jnp.exp(s - m_new)
            l_ref[h] = alpha * l_ref[h] + _col_sum(p)
            m_ref[h] = m_new
            ps.append(p.astype(BF16))
            alphas.append(alpha)
        for h in range(B_HEADS):
            vt = bvt_ref[c, h * B_DH:(h + 1) * B_DH, :]
            acc_ref[h] = alphas[h] * acc_ref[h] + _dot(vt, ps[h])
        return carry

    lax.fori_loop(0, n_chunks, att_chunk, 0)
    out_t = jnp.concatenate([acc_ref[h] / l_ref[h] for h in range(B_HEADS)], axis=0)
    o_ref[...] = out_t.T.astype(o_ref.dtype)


def _dsa_prompt(iqt, iwt, ik2, bqt, bk, bvt):
    nb, s, _ = bk.shape
    lc = bvt.shape[-1]
    tq = _pick_tile(s, 256, LANES)
    topk = min(TOPK_MAX, s // 4)
    nbits = max(1, (s - 1).bit_length())
    col = lambda r: pl.BlockSpec((None, r, tq), lambda b, i: (b, 0, i))
    seq = lambda w: pl.BlockSpec((None, s, w), lambda b, i: (b, 0, 0))
    return pl.pallas_call(
        functools.partial(_dsa_kernel, tq=tq, lc=lc, topk=topk, nbits=nbits),
        out_shape=jax.ShapeDtypeStruct((nb, s, B_HEADS * B_DH), BF16),
        grid=(nb, s // tq),
        in_specs=[col(SLAB), col(IDX_HEADS), seq(MISC), col(SLAB), seq(SLAB),
                  pl.BlockSpec((None, s // lc, SLAB, lc), lambda b, i: (b, 0, 0, 0))],
        out_specs=pl.BlockSpec((None, tq, B_HEADS * B_DH), lambda b, i: (b, i, 0)),
        scratch_shapes=[pltpu.VMEM((s // lc, lc, tq), jnp.int32),
                        pltpu.VMEM((s // lc, lc, tq), F32),
                        pltpu.VMEM((1, tq), jnp.int32),
                        pltpu.VMEM((B_HEADS, 1, tq), F32), pltpu.VMEM((B_HEADS, 1, tq), F32),
                        pltpu.VMEM((B_HEADS, B_DH, tq), F32)],
        compiler_params=_cparams("parallel", "arbitrary"),
        name="dsa_prompt",
    )(iqt, iwt, ik2, bqt, bk, bvt)


def _page_specs(n, shape, layer, pps):
    zeros = (0,) * len(shape)

    def mk(u):
        return pl.BlockSpec((None, None) + shape,
                            lambda b, g, pt, u=u: (layer, pt[b, g * pps + u]) + zeros)
    return [mk(u) for u in range(n)]


def _head_sums(x, groups):
    return jnp.concatenate(
        [jnp.sum(x[h * HEAD_DIM:(h + 1) * HEAD_DIM], axis=0, keepdims=True) for h in range(groups)],
        axis=0)


def _head_bcast(v):
    return jnp.concatenate(
        [jnp.broadcast_to(v[h:h + 1], (HEAD_DIM, v.shape[1])) for h in range(v.shape[0])], axis=0)


def _diff_sample_kernel(pt_ref, q_ref, kn_ref, vn_ref, lam_ref, gain_ref, *rest, pps, lam_init):
    kt_refs = rest[:pps]
    v_refs = rest[pps:2 * pps]
    o_ref, m_ref, l_ref, acc_ref, qb_ref = rest[2 * pps:]
    g = pl.program_id(1)
    nmap = 2 * A_HEADS

    @pl.when(g == 0)
    def _():
        m_ref[...] = jnp.full(m_ref.shape, NEG, F32)
        l_ref[...] = jnp.zeros(l_ref.shape, F32)
        acc_ref[...] = jnp.zeros(acc_ref.shape, F32)
        qb_ref[...] = jnp.broadcast_to(q_ref[...].astype(F32), qb_ref.shape)

    head_of_row = lax.broadcasted_iota(jnp.int32, (nmap, A_DV), 0) // 2
    s_rows = []
    for r in range(nmap):
        rows = slice(r * A_DK, (r + 1) * A_DK)
        qr = qb_ref[rows, :]
        s_rows.append(jnp.concatenate(
            [jnp.sum(kt_refs[u][rows, :] * qr, axis=0, keepdims=True) for u in range(pps)], axis=1))
    s = jnp.concatenate(s_rows, axis=0)
    m_prev = m_ref[...]
    m_new = jnp.maximum(m_prev, jnp.max(s, axis=-1, keepdims=True))
    alpha = jnp.exp(m_prev - m_new)
    p = jnp.exp(s - m_new)
    l_ref[...] = alpha * l_ref[...] + jnp.sum(p, axis=-1, keepdims=True)
    pv = jnp.zeros((nmap, A_DV), F32)
    for u in range(pps):
        pu = p[:, u * LANES:(u + 1) * LANES].astype(BF16)
        for hd in range(A_HEADS):
            vh = v_refs[u][pl.ds(hd, kt_refs[u].shape[-1], stride=A_HEADS), :].astype(BF16)
            pv = pv + jnp.where(head_of_row == hd, _dot(pu, vh), 0.0)
    acc_ref[...] = alpha * acc_ref[...] + pv
    m_ref[...] = m_new

    @pl.when(g == pl.num_programs(1) - 1)
    def _():
        s_self = _head_sums(q_ref[...].astype(F32) * kn_ref[...], nmap)
        m_prev = m_ref[...]
        m_fin = jnp.maximum(m_prev, s_self)
        alpha = jnp.exp(m_prev - m_fin)
        p_self = jnp.exp(s_self - m_fin)
        l_fin = alpha * l_ref[...] + p_self
        vn = vn_ref[...]
        vn8 = jnp.zeros((nmap, A_DV), F32)
        for hd in range(A_HEADS):
            vn8 = vn8 + jnp.where(head_of_row == hd, vn[:, hd * A_DV:(hd + 1) * A_DV], 0.0)
        o = (alpha * acc_ref[...] + p_self * vn8) / l_fin
        lam = _lambda(lam_ref, lam_init)
        gain = gain_ref[...]
        for hd in range(A_HEADS):
            dh = o[2 * hd:2 * hd + 1, :] - lam * o[2 * hd + 1:2 * hd + 2, :]
            y = dh * lax.rsqrt(jnp.mean(dh * dh, axis=-1, keepdims=True) + EPS) * gain
            o_ref[:, hd * A_DV:(hd + 1) * A_DV] = (y * (1.0 - lam_init)).astype(o_ref.dtype)


def _diff_sample(page_table, aq, ak_new, av_new, lamvec, gain, cache_kt, cache_v, layer, lam_init):
    ns, npg = page_table.shape
    pps = _pick_tile(npg, 8, 1)
    page = cache_kt.shape[-1]
    width = A_HEADS * A_DV
    per_b = lambda w: pl.BlockSpec((None, 1, w), lambda b, g, pt: (b, 0, 0))
    col_b = pl.BlockSpec((None, width, 1), lambda b, g, pt: (b, 0, 0))
    full = lambda a: pl.BlockSpec(a.shape, lambda b, g, pt: (0,) * a.ndim)
    grid_spec = pltpu.PrefetchScalarGridSpec(
        num_scalar_prefetch=1,
        grid=(ns, npg // pps),
        in_specs=[col_b, col_b, per_b(width), full(lamvec), full(gain)]
        + _page_specs(pps, (width, page), layer, pps)
        + _page_specs(pps, (page * A_HEADS, A_DV), layer, pps),
        out_specs=per_b(width),
        scratch_shapes=[pltpu.VMEM((2 * A_HEADS, 1), F32), pltpu.VMEM((2 * A_HEADS, 1), F32),
                        pltpu.VMEM((2 * A_HEADS, A_DV), F32), pltpu.VMEM((width, page), F32)],
    )
    return pl.pallas_call(
        functools.partial(_diff_sample_kernel, pps=pps, lam_init=lam_init),
        out_shape=jax.ShapeDtypeStruct((ns, 1, width), BF16),
        grid_spec=grid_spec,
        compiler_params=_cparams("parallel", "arbitrary"),
        name="diff_attn_sample",
    )(page_table, aq, ak_new, av_new, lamvec, gain, *([cache_kt] * pps), *([cache_v] * pps))


def _idx_sample_kernel(pt_ref, iq_ref, iw_ref, *rest, pps):
    ikt_refs = rest[:pps]
    o_ref = rest[pps]
    iq = iq_ref[...]
    w = iw_ref[...]
    rows = []
    for u in range(pps):
        lg = _dot(iq, ikt_refs[u][...].astype(BF16))
        rows.append(jnp.sum(jnp.maximum(lg, 0.0) * w, axis=0, keepdims=True) * IDX_SCALE)
    o_ref[...] = jnp.concatenate(rows, axis=0)


def _idx_sample(page_table, iq8, iw8, cache_ikt, layer):
    ns, npg = page_table.shape
    pps = _pick_tile(npg, 32, 8)
    page = cache_ikt.shape[-1]
    grid_spec = pltpu.PrefetchScalarGridSpec(
        num_scalar_prefetch=1,
        grid=(ns, npg // pps),
        in_specs=[pl.BlockSpec((None, IDX_HEADS, IDX_DIM), lambda b, g, pt: (b, 0, 0)),
                  pl.BlockSpec((None, IDX_HEADS, 1), lambda b, g, pt: (b, 0, 0))]
        + _page_specs(pps, (IDX_DIM, page), layer, pps),
        out_specs=pl.BlockSpec((None, pps, page), lambda b, g, pt: (b, g, 0)),
    )
    return pl.pallas_call(
        functools.partial(_idx_sample_kernel, pps=pps),
        out_shape=jax.ShapeDtypeStruct((ns, npg, page), F32),
        grid_spec=grid_spec,
        compiler_params=_cparams("parallel", "arbitrary"),
        name="idx_scores_sample",
    )(page_table, iq8, iw8, *([cache_ikt] * pps))


def _sample_select_kernel(sc_ref, iq_ref, misc_ref, bias_ref, bself_ref, *, topk, nbits, past):
    ns, n = sc_ref.shape
    misc = misc_ref[...]
    ik = misc[:, :IDX_DIM].astype(BF16).astype(F32)
    prod = iq_ref[...].astype(F32) * jnp.concatenate([ik] * IDX_HEADS, axis=1)
    grp = (lax.broadcasted_iota(jnp.int32, (IDX_HEADS * IDX_DIM, LANES), 0) // IDX_DIM
           == lax.broadcasted_iota(jnp.int32, (IDX_HEADS * IDX_DIM, LANES), 1)).astype(F32)
    lg = jnp.dot(prod, grp, precision=HIGHEST, preferred_element_type=F32)
    s_self = jnp.sum(jnp.maximum(lg[:, :IDX_HEADS], 0.0) * misc[:, IW_LANE0:IW_LANE0 + IDX_HEADS],
                     axis=1, keepdims=True) * IDX_SCALE
    key_self = _sort_key(s_self)
    key = _sort_key(sc_ref[...])
    kpos = lax.broadcasted_iota(jnp.int32, key.shape, 1)

    def total(mask, self_mask):
        return jnp.sum(jnp.where(mask, 1, 0), axis=1, keepdims=True) + jnp.where(self_mask, 1, 0)

    thr = _kth_largest(lambda cand: total(key >= cand, key_self >= cand), topk, (ns, 1))
    need = topk - total(key > thr, key_self > thr)
    jstar = _tie_cut(
        lambda cand: total(jnp.logical_and(key == thr, kpos < cand),
                           jnp.logical_and(key_self == thr, past < cand)),
        need, nbits, (ns, 1))
    bias_ref[...] = jnp.where(
        key > thr, 0.0, jnp.where(key == thr, jnp.where(kpos <= jstar, 0.0, NEG), NEG))
    bself_ref[...] = jnp.where(
        key_self > thr, 0.0, jnp.where(key_self == thr, jnp.where(past <= jstar, 0.0, NEG), NEG))


def _sample_select(scores2d, iq2d, misc2d, past):
    ns, n = scores2d.shape
    topk = min(TOPK_MAX, (past + 1) // 4)
    nbits = max(1, past.bit_length())
    return pl.pallas_call(
        functools.partial(_sample_select_kernel, topk=topk, nbits=nbits, past=past),
        out_shape=(jax.ShapeDtypeStruct((ns, n), F32), jax.ShapeDtypeStruct((ns, 1), F32)),
        compiler_params=pltpu.CompilerParams(vmem_limit_bytes=VMEM_LIMIT),
        name="sample_select",
    )(scores2d, iq2d, misc2d)


def _dsa_sample_kernel(pt_ref, bias_ref, bself_ref, q_ref, kn_ref, vn_ref, *rest, pps):
    kt_refs = rest[:pps]
    vt_refs = rest[pps:2 * pps]
    o_ref, m_ref, l_ref, acc_ref, qb_ref = rest[2 * pps:]
    g = pl.program_id(1)
    page = kt_refs[0].shape[-1]

    @pl.when(g == 0)
    def _():
        m_ref[...] = jnp.full(m_ref.shape, NEG, F32)
        l_ref[...] = jnp.zeros(l_ref.shape, F32)
        acc_ref[...] = jnp.zeros(acc_ref.shape, F32)
        qb_ref[...] = jnp.broadcast_to(q_ref[...].astype(F32), qb_ref.shape)

    rows = lambda h: slice(h * B_DH, (h + 1) * B_DH)
    s_rows = []
    for h in range(B_HEADS):
        qh = qb_ref[rows(h), :]
        s_rows.append(jnp.concatenate(
            [jnp.sum(kt_refs[u][rows(h), :] * qh, axis=0, keepdims=True) for u in range(pps)], axis=1))
    bias = jnp.concatenate([bias_ref[pl.ds(g * pps + u, 1), :] for u in range(pps)], axis=1)
    s = jnp.concatenate(s_rows, axis=0) + bias
    m_prev = m_ref[...]
    m_new = jnp.maximum(m_prev, jnp.max(s, axis=-1, keepdims=True))
    alpha = jnp.exp(m_prev - m_new)
    p = jnp.exp(s - m_new)
    l_ref[...] = alpha * l_ref[...] + jnp.sum(p, axis=-1, keepdims=True)
    m_ref[...] = m_new
    for h in range(B_HEADS):
        acc_h = acc_ref[rows(h), :] * alpha[h:h + 1, :]
        for u in range(pps):
            acc_h = acc_h + vt_refs[u][rows(h), :] * p[h:h + 1, u * page:(u + 1) * page]
        acc_ref[rows(h), :] = acc_h

    @pl.when(g == pl.num_programs(1) - 1)
    def _():
        s_self = _head_sums(q_ref[...].astype(F32) * kn_ref[...], B_HEADS) + bself_ref[...]
        m_prev = m_ref[...]
        m_fin = jnp.maximum(m_prev, s_self)
        alpha = jnp.exp(m_prev - m_fin)
        p_self = jnp.exp(s_self - m_fin)
        l_fin = alpha * l_ref[...] + p_self
        acc_sum = jnp.sum(acc_ref[...], axis=1, keepdims=True)
        o = (_head_bcast(alpha) * acc_sum + _head_bcast(p_self) * vn_ref[...]) / _head_bcast(l_fin)
        o_ref[...] = o.astype(o_ref.dtype)


def _dsa_sample(page_table, bias, bself, bq, bk_new, bv_new, cache_kt, cache_vt, layer):
    ns, npg = page_table.shape
    pps = _pick_tile(npg, 16, 1)
    page = cache_kt.shape[-1]
    width = B_HEADS * B_DH
    per_b = lambda r, w: pl.BlockSpec((None, r, w), lambda b, g, pt: (b, 0, 0))
    grid_spec = pltpu.PrefetchScalarGridSpec(
        num_scalar_prefetch=1,
        grid=(ns, npg // pps),
        in_specs=[per_b(npg, page), per_b(1, 1), per_b(width, 1), per_b(width, 1), per_b(width, 1)]
        + _page_specs(pps, (width, page), layer, pps) + _page_specs(pps, (width, page), layer, pps),
        out_specs=per_b(width, 1),
        scratch_shapes=[pltpu.VMEM((B_HEADS, 1), F32), pltpu.VMEM((B_HEADS, 1), F32),
                        pltpu.VMEM((width, page), F32), pltpu.VMEM((width, page), F32)],
    )
    return pl.pallas_call(
        functools.partial(_dsa_sample_kernel, pps=pps),
        out_shape=jax.ShapeDtypeStruct((ns, width, 1), BF16),
        grid_spec=grid_spec,
        compiler_params=_cparams("parallel", "arbitrary"),
        name="dsa_sample",
    )(page_table, bias, bself, bq, bk_new, bv_new, *([cache_kt] * pps), *([cache_vt] * pps))


def _outproj_kernel(*refs, route):
    if route:
        (ao_ref, bo_ref, w_ref, x_ref, g1_ref, sc_ref, sh_ref, gf_ref, wr_ref, br_ref,
         x1_ref, h2_ref, comb_ref) = refs
    else:
        ao_ref, bo_ref, w_ref, x_ref, g1_ref, sc_ref, sh_ref, gf_ref, x1_ref, h2_ref = refs
    wa = ao_ref.shape[-1]
    mix = _dot(ao_ref[...], w_ref[:wa, :]) + _dot(bo_ref[...], w_ref[wa:, :])
    x1 = x_ref[...] + g1_ref[...] * mix
    x1_ref[...] = x1
    h = x1 * lax.rsqrt(jnp.mean(x1 * x1, axis=-1, keepdims=True) + EPS) * gf_ref[...]
    h = h * (1.0 + sc_ref[...]) + sh_ref[...]
    h2_ref[...] = h.astype(BF16)
    if route:
        logits = jnp.dot(h, wr_ref[...], precision=HIGHEST, preferred_element_type=F32) + br_ref[...]
        lane = lax.broadcasted_iota(jnp.int32, logits.shape, 1)
        logits = jnp.where(lane < N_EXPERTS, logits, -jnp.inf)
        v1 = jnp.max(logits, axis=-1, keepdims=True)
        i1 = jnp.min(jnp.where(logits == v1, lane, LANES), axis=-1, keepdims=True)
        rest = jnp.where(lane == i1, -jnp.inf, logits)
        v2 = jnp.max(rest, axis=-1, keepdims=True)
        i2 = jnp.min(jnp.where(rest == v2, lane, LANES), axis=-1, keepdims=True)
        e2 = jnp.exp(v2 - v1)
        den = 1.0 + e2
        comb_ref[...] = jnp.where(lane == i1, 1.0 / den, jnp.where(lane == i2, e2 / den, 0.0))


def _outproj(ao, bo, w_bf, x3, gate1, scale2, shift2, gf, router=None):
    nb, s, d = x3.shape
    tm = _pick_tile(s, 512, 16)
    row = lambda w: pl.BlockSpec((None, tm, w), lambda b, i: (b, i, 0))
    full = lambda a: pl.BlockSpec(a.shape, lambda b, i: (0,) * a.ndim)
    ins = [ao, bo, w_bf, x3, gate1, scale2, shift2, gf]
    in_specs = [row(ao.shape[-1]), row(bo.shape[-1]), full(w_bf), row(d), _mod_spec(gate1, tm),
                _mod_spec(scale2, tm), _mod_spec(shift2, tm), full(gf)]
    out_shape = [jax.ShapeDtypeStruct((nb, s, d), F32), jax.ShapeDtypeStruct((nb, s, d), BF16)]
    out_specs = [row(d), row(d)]
    if router is not None:
        ins += list(router)
        in_specs += [full(router[0]), full(router[1])]
        out_shape.append(jax.ShapeDtypeStruct((nb, s, LANES), F32))
        out_specs.append(row(LANES))
    return pl.pallas_call(
        functools.partial(_outproj_kernel, route=router is not None),
        out_shape=tuple(out_shape),
        grid=(nb, s // tm),
        in_specs=in_specs,
        out_specs=tuple(out_specs),
        compiler_params=_cparams("parallel", "parallel"),
        name="out_proj",
    )(*ins)


def _ffn_kernel(h_ref, w1_ref, w3_ref, w2_ref, x_ref, g2_ref, o_ref, acc_ref):
    j = pl.program_id(2)

    @pl.when(j == 0)
    def _():
        acc_ref[...] = jnp.zeros(acc_ref.shape, F32)

    h = h_ref[...]
    t = jax.nn.silu(_dot(h, w1_ref[...])) * _dot(h, w3_ref[...])
    acc_ref[...] += _dot(t.astype(BF16), w2_ref[...])

    @pl.when(j == pl.num_programs(2) - 1)
    def _():
        o_ref[...] = x_ref[...] + g2_ref[...] * acc_ref[...]


def _ffn_dense(h2, w1, w3, w2, x1, gate2):
    nb, s, d = x1.shape
    ff = w1.shape[1]
    tm = _pick_tile(s, 512, 16)
    tf = _pick_tile(ff, 1408, LANES)
    row = pl.BlockSpec((None, tm, d), lambda b, i, j: (b, i, 0))
    return pl.pallas_call(
        _ffn_kernel,
        out_shape=jax.ShapeDtypeStruct((nb, s, d), F32),
        grid=(nb, s // tm, ff // tf),
        in_specs=[row, pl.BlockSpec((d, tf), lambda b, i, j: (0, j)),
                  pl.BlockSpec((d, tf), lambda b, i, j: (0, j)),
                  pl.BlockSpec((tf, d), lambda b, i, j: (j, 0)), row, _mod_spec(gate2, tm)],
        out_specs=row,
        scratch_shapes=[pltpu.VMEM((tm, d), F32)],
        compiler_params=_cparams("parallel", "parallel", "arbitrary"),
        name="ffn_dense",
    )(h2, w1, w3, w2, x1, gate2)


def _moe_kernel(h_ref, comb_ref, w1_ref, w3_ref, w2_ref, x_ref, g2_ref, o_ref, acc_ref):
    e = pl.program_id(2)
    j = pl.program_id(3)

    @pl.when(jnp.logical_and(e == 0, j == 0))
    def _():
        acc_ref[...] = jnp.zeros(acc_ref.shape, F32)

    comb = comb_ref[...]
    lane = lax.broadcasted_iota(jnp.int32, comb.shape, 1)
    ce = jnp.sum(jnp.where(lane == e, comb, 0.0), axis=-1, keepdims=True)
    h = h_ref[...]
    t = jax.nn.silu(_dot(h, w1_ref[...])) * _dot(h, w3_ref[...])
    acc_ref[...] += ce * _dot(t.astype(BF16), w2_ref[...])

    @pl.when(jnp.logical_and(e == pl.num_programs(2) - 1, j == pl.num_programs(3) - 1))
    def _():
        o_ref[...] = x_ref[...] + g2_ref[...] * acc_ref[...]


def _ffn_moe(h2, comb, w1, w3, w2, x1, gate2):
    nb, s, d = x1.shape
    ne, _, ff = w1.shape
    tm = _pick_tile(s, 512, 16)
    tf = _pick_tile(ff, 1792, LANES)
    row = lambda w: pl.BlockSpec((None, tm, w), lambda b, i, e, j: (b, i, 0))
    return pl.pallas_call(
        _moe_kernel,
        out_shape=jax.ShapeDtypeStruct((nb, s, d), F32),
        grid=(nb, s // tm, ne, ff // tf),
        in_specs=[row(d), row(LANES),
                  pl.BlockSpec((None, d, tf), lambda b, i, e, j: (e, 0, j)),
                  pl.BlockSpec((None, d, tf), lambda b, i, e, j: (e, 0, j)),
                  pl.BlockSpec((None, tf, d), lambda b, i, e, j: (e, j, 0)), row(d),
                  _mod_spec(gate2, tm)],
        out_specs=row(d),
        scratch_shapes=[pltpu.VMEM((tm, d), F32)],
        compiler_params=_cparams("parallel", "parallel", "arbitrary", "arbitrary"),
        name="ffn_moe",
    )(h2, comb, w1, w3, w2, x1, gate2)


def kernel(x_prompt, x_sample, cache_a_k, cache_a_v, cache_b_k, cache_b_v, cache_b_ik, page_table, c_prompt, c_sample, w_ada, b_ada, g_mix, g_ffn, w_in, w_out, a_q_norm, a_k_norm, b_q_norm, b_k_norm, lam_q1, lam_k1, lam_q2, lam_k2, a_sub_norm, w1_dense, w3_dense, w2_dense, w_router, b_router, w1_exp, w3_exp, w2_exp):
    nb, s, d = x_prompt.shape
    ns, dec_seq, _ = x_sample.shape
    assert dec_seq == 1
    depth = w_in.shape[0]
    n_pool, page = cache_a_k.shape[1:3]
    past = page_table.shape[1] * page
    assert w_in.shape[2] == N_SLABS * SLAB + IDX_DIM + IDX_HEADS

    rows = nb + ns
    rpad = -rows % 8
    c_all = jnp.concatenate([c_prompt, c_sample, jnp.zeros((rpad, d), F32)], axis=0)
    mod = _ada(c_all, w_ada, b_ada)

    tabs_p = _rope_tables(jnp.arange(s, dtype=jnp.int32))
    tabs_s = _rope_tables(jnp.full((ns,), past, jnp.int32))
    gidx = jnp.arange(SLAB) // HEAD_DIM
    m64 = (gidx[:, None] == gidx[None, :]).astype(BF16) * (1.0 / HEAD_DIM)

    ca_kt = jnp.transpose(cache_a_k, (0, 1, 3, 4, 5, 2)).reshape(depth, n_pool, A_HEADS * 2 * A_DK, page)
    ca_v = cache_a_v.reshape(depth, n_pool, page * A_HEADS, A_DV)
    cb_kt = jnp.transpose(cache_b_k, (0, 1, 3, 4, 2)).reshape(depth, n_pool, B_HEADS * B_DH, page)
    cb_vt = jnp.transpose(cache_b_v, (0, 1, 3, 4, 2)).reshape(depth, n_pool, B_HEADS * B_DH, page)
    cb_ikt = jnp.transpose(cache_b_ik, (0, 1, 3, 2))

    xp = x_prompt
    xs = x_sample.reshape(1, ns, d)
    outs_p, outs_s = [], []
    for l in range(depth):
        lam_init = 0.8 - 0.6 * math.exp(-0.3 * l)
        w_in_bf = jnp.pad(w_in[l], ((0, 0), (0, NPROJ - w_in.shape[2]))).astype(BF16)
        w_out_bf = w_out[l].astype(BF16)
        rep = SLAB // HEAD_DIM
        gains = jnp.stack([jnp.tile(a_q_norm[l], rep), jnp.tile(a_k_norm[l], rep),
                           jnp.tile(b_q_norm[l], rep), jnp.tile(b_k_norm[l], rep)])
        lamvec = jnp.stack([lam_q1[l], lam_k1[l], lam_q2[l], lam_k2[l]])
        sub_gain = a_sub_norm[l].reshape(1, A_DV)
        sub_gain_col = a_sub_norm[l].reshape(A_DV, 1)
        g_mix_l = g_mix[l].reshape(1, d)
        g_ffn_l = g_ffn[l].reshape(1, d)
        mods_p = [m.reshape(nb, 1, d) for m in jnp.split(mod[l, :nb], 6, axis=-1)]
        mods_s = [m.reshape(1, ns, d) for m in jnp.split(mod[l, nb:rows], 6, axis=-1)]
        moe = l % 2 == 1
        jj = l // 2
        if moe:
            router = (jnp.pad(w_router[jj], ((0, 0), (0, LANES - N_EXPERTS))),
                      jnp.pad(b_router[jj], (0, LANES - N_EXPERTS)).reshape(1, LANES))
            ffw = (w1_exp[jj].astype(BF16), w3_exp[jj].astype(BF16), w2_exp[jj].astype(BF16))
        else:
            router = None
            ffw = (w1_dense[jj].astype(BF16), w3_dense[jj].astype(BF16), w2_dense[jj].astype(BF16))

        def mix_and_ffn(x3, mods, ao, bo):
            shift1, scale1, gate1, shift2, scale2, gate2 = mods
            res = _outproj(ao, bo, w_out_bf, x3, gate1, scale2, shift2, g_ffn_l, router)
            if moe:
                x1, h2, comb = res
                return _ffn_moe(h2, comb, *ffw, x1, gate2)
            x1, h2 = res
            return _ffn_dense(h2, *ffw, x1, gate2)

        shift1, scale1 = mods_p[0], mods_p[1]
        (akf, avf, bkf, bvf, miscf, akb, bkb, ik2b, aqt, avt, bqt, bvt, iqt, iwt) = _project(
            xp, scale1, shift1, g_mix_l, w_in_bf, gains, tabs_p, m64, True)
        ao = _diff_prompt(aqt, akb, avt, lamvec, sub_gain_col, lam_init)
        bo = _dsa_prompt(iqt, iwt, ik2b, bqt, bkb, bvt)
        xp = mix_and_ffn(xp, mods_p, ao, bo)
        outs_p.append((akf.reshape(nb, s, A_HEADS, 2, A_DK), avf.reshape(nb, s, A_HEADS, A_DV),
                       bkf.reshape(nb, s, B_HEADS, B_DH), bvf.reshape(nb, s, B_HEADS, B_DH),
                       miscf[..., :IDX_DIM]))

        shift1, scale1 = mods_s[0], mods_s[1]
        (akf, avf, bkf, bvf, miscf, aqb, bqb, iqb) = _project(
            xs, scale1, shift1, g_mix_l, w_in_bf, gains, tabs_s, m64, False)
        per_tok = lambda a: a.reshape(ns, 1, a.shape[-1])
        as_col = lambda a: a.reshape(ns, a.shape[-1], 1)
        ao = _diff_sample(page_table, as_col(aqb), as_col(akf), per_tok(avf), lamvec, sub_gain,
                          ca_kt, ca_v, l, lam_init)
        iq8 = iqb.reshape(ns, IDX_HEADS, IDX_DIM)
        iw8 = miscf[0, :, IW_LANE0:IW_LANE0 + IDX_HEADS].reshape(ns, IDX_HEADS, 1)
        scores = _idx_sample(page_table, iq8, iw8, cb_ikt, l)
        bias, bself = _sample_select(scores.reshape(ns, past), iqb[0], miscf[0], past)
        as_col = lambda a: a.reshape(ns, a.shape[-1], 1)
        bo = _dsa_sample(page_table, bias.reshape(scores.shape), bself.reshape(ns, 1, 1),
                         as_col(bqb), as_col(bkf), as_col(bvf), cb_kt, cb_vt, l)
        xs = mix_and_ffn(xs, mods_s, ao.reshape(1, ns, -1), bo.reshape(1, ns, -1))
        outs_s.append((akf.reshape(ns, 1, A_HEADS, 2, A_DK), avf.reshape(ns, 1, A_HEADS, A_DV),
                       bkf.reshape(ns, 1, B_HEADS, B_DH), bvf.reshape(ns, 1, B_HEADS, B_DH),
                       miscf[0, :, :IDX_DIM].reshape(ns, 1, IDX_DIM)))

    stack = lambda rows_, i: jnp.stack([r[i] for r in rows_])
    return (xp, xs.reshape(ns, 1, d),
            stack(outs_p, 0), stack(outs_p, 1), stack(outs_p, 2), stack(outs_p, 3), stack(outs_p, 4),
            stack(outs_s, 0), stack(outs_s, 1), stack(outs_s, 2), stack(outs_s, 3), stack(outs_s, 4))
```
